```python
import jax, jax.numpy as jnp
from jax import lax
import numpy as np

D_MODEL = 4096
BATCH = 1
SEQ = 16384
DEPTH = 2

CHUNK = 64
N_MIXERS = 2
D_LRU = D_MODEL
LRU_HEADS = 16
LRU_HEAD_DIM = D_LRU // LRU_HEADS
CONV_WIDTH = 4
LRU_C = 8.0
POOL_WINDOWS = (2, 4, 8, 16)
POOL_GROUPS = len(POOL_WINDOWS)
POOL_GROUP_DIM = D_MODEL // POOL_GROUPS
MAX_WINDOW = 16
N_EXPERTS = 64
TOP_K = 8
N_GROUPS = 8
TOPK_GROUPS = 4
D_EXPERT = 384
D_SHARED = 384
ROUTED_SCALE = 2.5
EXPERT_BLOCK = 256
EPS = 1e-6

kernel_name = "hybrid_rglru_pool_moe_adaln"


def rmsnorm(x):
    x32 = x.astype(jnp.float32)
    y = x32 * lax.rsqrt(jnp.mean(x32 * x32, axis=-1, keepdims=True) + EPS)
    return y.astype(x.dtype)


def adaln(c, w, b):
    mod = jax.nn.silu(c) @ w + b
    shift, scale, gate = jnp.split(mod, 3, axis=-1)
    return shift[:, None, :], scale[:, None, :], gate[:, None, :]


def _combine(left, right):
    a_l, b_l = left
    a_r, b_r = right
    return a_l * a_r, a_r * b_l + b_r


def chunked_linear_scan(a, b):
    bsz, seq, d = a.shape
    nc = seq // CHUNK
    a_c = a.reshape(bsz, nc, CHUNK, d)
    b_c = b.reshape(bsz, nc, CHUNK, d)
    a_cum, h_loc = lax.associative_scan(_combine, (a_c, b_c), axis=2)

    def step(h_prev, xs):
        a_last, h_last = xs
        return a_last * h_prev + h_last, h_prev

    xs = (jnp.moveaxis(a_cum[:, :, -1], 1, 0), jnp.moveaxis(h_loc[:, :, -1], 1, 0))
    _, h_start = lax.scan(step, jnp.zeros((bsz, d), a.dtype), xs)
    h = h_loc + a_cum * jnp.moveaxis(h_start, 0, 1)[:, :, None, :]
    return h.reshape(bsz, seq, d)


def rglru_mixer(h, w_in, b_in, conv_w, conv_b, ga_w, ga_b, gx_w, gx_b, lam, w_out, b_out):
    bsz, seq, _ = h.shape
    proj = h @ w_in + b_in
    gate_branch, u = jnp.split(proj, 2, axis=-1)
    g = jax.nn.gelu(gate_branch, approximate=True)
    u_pad = jnp.pad(u, ((0, 0), (CONV_WIDTH - 1, 0), (0, 0)))
    xc = conv_b + conv_w[0] * u_pad[:, 0:seq]
    for k in range(1, CONV_WIDTH):
        xc = xc + conv_w[k] * u_pad[:, k:k + seq]
    xh = xc.reshape(bsz, seq, LRU_HEADS, LRU_HEAD_DIM)
    r = jax.nn.sigmoid(jnp.einsum('bshd,hde->bshe', xh, ga_w).reshape(bsz, seq, D_LRU) + ga_b)
    i = jax.nn.sigmoid(jnp.einsum('bshd,hde->bshe', xh, gx_w).reshape(bsz, seq, D_LRU) + gx_b)
    log_a = LRU_C * r.astype(jnp.float32) * jax.nn.log_sigmoid(lam.astype(jnp.float32))
    a = jnp.exp(log_a)
    mult = jnp.sqrt(-jnp.expm1(2.0 * log_a))
    b = mult * (i * xc).astype(jnp.float32)
    hs = chunked_linear_scan(a, b).astype(h.dtype)
    return (hs * g) @ w_out + b_out


def pool_mixer(h, pool_w, pool_b, pool_scale):
    seq = h.shape[1]
    h32 = h.astype(jnp.float32)
    cs = jnp.cumsum(jnp.pad(h32, ((0, 0), (MAX_WINDOW, 0), (0, 0))), axis=1)
    pos1 = jnp.arange(1, seq + 1)
    outs = []
    for g, w in enumerate(POOL_WINDOWS):
        lo, hi = g * POOL_GROUP_DIM, (g + 1) * POOL_GROUP_DIM
        win_sum = (cs[:, MAX_WINDOW:MAX_WINDOW + seq, lo:hi]
                   - cs[:, MAX_WINDOW - w:MAX_WINDOW - w + seq, lo:hi])
        cnt = jnp.minimum(pos1, w).astype(jnp.float32)[None, :, None]
        pooled = (win_sum / cnt - h32[:, :, lo:hi]).astype(h.dtype)
        outs.append(pooled @ pool_w[g] + pool_b[lo:hi])
    return jnp.concatenate(outs, axis=-1) * pool_scale


def swiglu(x, w_gate, w_up, w_down):
    return (jax.nn.silu(x @ w_gate) * (x @ w_up)) @ w_down


def moe(h, router_w, router_bias, w_gate, w_up, w_down, ws_gate, ws_up, ws_down):
    bsz, seq, d = h.shape
    n = bsz * seq
    hf = h.reshape(n, d)
    scores = jax.nn.sigmoid((hf @ router_w).astype(jnp.float32))
    biased = scores + router_bias.astype(jnp.float32)
    grp = biased.reshape(n, N_GROUPS, N_EXPERTS // N_GROUPS)
    grp_score = lax.top_k(grp, 2)[0].sum(-1)
    _, gidx = lax.top_k(grp_score, TOPK_GROUPS)
    gmask = jax.nn.one_hot(gidx, N_GROUPS, dtype=jnp.float32).sum(1) > 0.0
    emask = jnp.repeat(gmask, N_EXPERTS // N_GROUPS, axis=1)
    _, idx = lax.top_k(jnp.where(emask, biased, -jnp.inf), TOP_K)
    wts = jnp.take_along_axis(scores, idx, axis=1)
    wts = wts / jnp.sum(wts, axis=-1, keepdims=True) * ROUTED_SCALE

    n_assign = n * TOP_K
    n_blocks = (n_assign + N_EXPERTS * (EXPERT_BLOCK - 1) + EXPERT_BLOCK - 1) // EXPERT_BLOCK
    n_rows = n_blocks * EXPERT_BLOCK
    e_flat = idx.reshape(-1)
    tok_flat = jnp.arange(n_assign, dtype=jnp.int32) // TOP_K
    w_flat = wts.reshape(-1)
    order = jnp.argsort(e_flat)
    e_sorted = e_flat[order]
    counts = jnp.bincount(e_flat, length=N_EXPERTS)
    start = jnp.cumsum(counts) - counts
    padded = (counts + EXPERT_BLOCK - 1) // EXPERT_BLOCK * EXPERT_BLOCK
    pad_end = jnp.cumsum(padded)
    pad_start = pad_end - padded
    dest = pad_start[e_sorted] + jnp.arange(n_assign, dtype=jnp.int32) - start[e_sorted]
    row_tok = jnp.zeros((n_rows,), jnp.int32).at[dest].set(tok_flat[order])
    row_w = jnp.zeros((n_rows,), jnp.float32).at[dest].set(w_flat[order])
    block_exp = jnp.minimum(
        jnp.searchsorted(pad_end, jnp.arange(n_blocks, dtype=jnp.int32) * EXPERT_BLOCK, side='right'),
        N_EXPERTS - 1)

    def block_step(acc, xs):
        e, tok, w = xs
        xb = hf[tok]
        yb = swiglu(xb, w_gate[e], w_up[e], w_down[e]).astype(jnp.float32) * w[:, None]
        return acc.at[tok].add(yb), None

    routed, _ = lax.scan(block_step, jnp.zeros((n, d), jnp.float32),
                         (block_exp, row_tok.reshape(n_blocks, EXPERT_BLOCK),
                          row_w.reshape(n_blocks, EXPERT_BLOCK)))
    out = routed.astype(h.dtype) + swiglu(hf, ws_gate, ws_up, ws_down)
    return out.reshape(bsz, seq, d)


def setup_inputs(seed: int = 0) -> dict:
    key = jax.random.key(seed)
    ks = jax.random.split(key, 28)
    f32 = jnp.float32
    n_a = (DEPTH + 1) // 2
    n_b = DEPTH // 2
    D = D_MODEL

    def nrm(k, shape, std):
        return std * jax.random.normal(k, shape, f32)

    u = jax.random.uniform(ks[12], (n_a, D_LRU), f32, minval=0.9, maxval=0.999)
    a_base = u ** (1.0 / LRU_C)
    lam = jnp.log(a_base) - jnp.log1p(-a_base)
    return {
        "x": nrm(ks[0], (BATCH, SEQ, D), 1.0),
        "c": nrm(ks[1], (BATCH, D), 1.0),
        "mod_w": nrm(ks[2], (DEPTH, 2, D, 3 * D), 0.5 * D ** -0.5),
        "mod_b": nrm(ks[3], (DEPTH, 2, 3 * D), 0.02),
        "lru_w_in": nrm(ks[4], (n_a, D, 2 * D_LRU), D ** -0.5),
        "lru_b_in": nrm(ks[5], (n_a, 2 * D_LRU), 0.02),
        "lru_conv_w": nrm(ks[6], (n_a, CONV_WIDTH, D_LRU), CONV_WIDTH ** -0.5),
        "lru_conv_b": nrm(ks[7], (n_a, D_LRU), 0.02),
        "lru_gate_a_w": nrm(ks[8], (n_a, LRU_HEADS, LRU_HEAD_DIM, LRU_HEAD_DIM), LRU_HEAD_DIM ** -0.5),
        "lru_gate_a_b": nrm(ks[9], (n_a, D_LRU), 0.02),
        "lru_gate_x_w": nrm(ks[10], (n_a, LRU_HEADS, LRU_HEAD_DIM, LRU_HEAD_DIM), LRU_HEAD_DIM ** -0.5),
        "lru_gate_x_b": nrm(ks[11], (n_a, D_LRU), 0.02),
        "lru_lambda": lam,
        "lru_w_out": nrm(ks[13], (n_a, D_LRU, D), D_LRU ** -0.5),
        "lru_b_out": nrm(ks[14], (n_a, D), 0.02),
        "pool_w": nrm(ks[15], (n_b, POOL_GROUPS, POOL_GROUP_DIM, POOL_GROUP_DIM), POOL_GROUP_DIM ** -0.5),
        "pool_b": nrm(ks[16], (n_b, D), 0.02),
        "pool_scale": 1.0 + nrm(ks[17], (n_b, D), 0.02),
        "router_w": nrm(ks[18], (DEPTH, D, N_EXPERTS), D ** -0.5),
        "router_bias": nrm(ks[19], (DEPTH, N_EXPERTS), 0.01),
        "expert_w_gate": nrm(ks[20], (DEPTH, N_EXPERTS, D, D_EXPERT), D ** -0.5),
        "expert_w_up": nrm(ks[21], (DEPTH, N_EXPERTS, D, D_EXPERT), D ** -0.5),
        "expert_w_down": nrm(ks[22], (DEPTH, N_EXPERTS, D_EXPERT, D), D_EXPERT ** -0.5),
        "shared_w_gate": nrm(ks[23], (DEPTH, D, D_SHARED), D ** -0.5),
        "shared_w_up": nrm(ks[24], (DEPTH, D, D_SHARED), D ** -0.5),
        "shared_w_down": nrm(ks[25], (DEPTH, D_SHARED, D), D_SHARED ** -0.5),
        "final_gain": 1.0 + nrm(ks[26], (D,), 0.02),
    }


def reference(x, c, mod_w, mod_b, lru_w_in, lru_b_in, lru_conv_w, lru_conv_b,
              lru_gate_a_w, lru_gate_a_b, lru_gate_x_w, lru_gate_x_b, lru_lambda,
              lru_w_out, lru_b_out, pool_w, pool_b, pool_scale, router_w, router_bias,
              expert_w_gate, expert_w_up, expert_w_down, shared_w_gate, shared_w_up,
              shared_w_down, final_gain):
    for i in range(DEPTH):
        j = i // N_MIXERS
        shift, scale, gate = adaln(c, mod_w[i, 0], mod_b[i, 0])
        h = rmsnorm(x) * (1.0 + scale) + shift
        if i % N_MIXERS == 0:
            y = rglru_mixer(h, lru_w_in[j], lru_b_in[j], lru_conv_w[j], lru_conv_b[j],
                            lru_gate_a_w[j], lru_gate_a_b[j], lru_gate_x_w[j], lru_gate_x_b[j],
                            lru_lambda[j], lru_w_out[j], lru_b_out[j])
        else:
            y = pool_mixer(h, pool_w[j], pool_b[j], pool_scale[j])
        x = x + gate * y
        shift, scale, gate = adaln(c, mod_w[i, 1], mod_b[i, 1])
        h = rmsnorm(x) * (1.0 + scale) + shift
        y = moe(h, router_w[i], router_bias[i], expert_w_gate[i], expert_w_up[i],
                expert_w_down[i], shared_w_gate[i], shared_w_up[i], shared_w_down[i])
        x = x + gate * y
    return rmsnorm(x) * final_gain
```

```python
import functools

import jax
import jax.numpy as jnp
from jax import lax
from jax.experimental import pallas as pl
from jax.experimental.pallas import tpu as pltpu

F32 = jnp.float32
BF16 = jnp.bfloat16
U32 = jnp.uint32
I32 = jnp.int32

EPS = 1e-6
LRU_C = 8.0
N_GROUPS = 8
TOPK_GROUPS = 4
TOP_K = 8
ROUTED_SCALE = 2.5
POOL_WINDOWS = (2, 4, 8, 16)
EXPERT_BLOCK = 256
POOL_HIST = 24
VMEM_LIMIT = 56 * 1024 * 1024


def _cparams(sem, vmem=VMEM_LIMIT):
    return pltpu.CompilerParams(dimension_semantics=sem, vmem_limit_bytes=vmem)


def _pack_pair(lo, hi):
    lo_b = lax.bitcast_convert_type(lo.astype(BF16).astype(F32), U32)
    hi_b = lax.bitcast_convert_type(hi.astype(BF16).astype(F32), U32)
    return hi_b | (lo_b >> 16)


def _unpack_pair(w):
    lo = lax.bitcast_convert_type(w << 16, F32)
    hi = lax.bitcast_convert_type(w & jnp.uint32(0xFFFF0000), F32)
    return lo, hi


def _rms_inv(ss, d):
    return lax.rsqrt(ss / d + EPS)


def _mod_kernel(c_ref, w_ref, b_ref, o_ref):
    c = c_ref[...]
    s = c * jax.nn.sigmoid(c)
    s8 = jnp.broadcast_to(s, (8, s.shape[1])).astype(BF16)
    acc = jnp.dot(s8, w_ref[...].astype(BF16), preferred_element_type=F32)
    o_ref[...] = acc[0:1, :] + b_ref[...]


def _adaln_all(c, mod_w, mod_b):
    depth, two, d, d3 = mod_w.shape
    na = depth * two
    tn = 512
    w = mod_w.reshape(na, d, d3)
    b = mod_b.reshape(na, 1, d3)
    return pl.pallas_call(
        _mod_kernel,
        out_shape=jax.ShapeDtypeStruct((na, 1, d3), F32),
        grid=(na, d3 // tn),
        in_specs=[
            pl.BlockSpec((1, d), lambda a, j: (0, 0)),
            pl.BlockSpec((None, d, tn), lambda a, j: (a, 0, j)),
            pl.BlockSpec((None, 1, tn), lambda a, j: (a, 0, j)),
        ],
        out_specs=pl.BlockSpec((None, 1, tn), lambda a, j: (a, 0, j)),
        compiler_params=_cparams(("parallel", "parallel")),
        name="adaln_mod",
    )(c, w, b)


def _norm_mod_kernel(x_ref, sc_ref, sh_ref, o_ref):
    x = x_ref[...]
    inv = _rms_inv(jnp.sum(x * x, axis=-1, keepdims=True), x.shape[-1])
    o_ref[...] = ((x * inv) * (1.0 + sc_ref[...]) + sh_ref[...]).astype(o_ref.dtype)


def _norm_mod(x, scale, shift):
    n, d = x.shape
    tm = min(512, n)
    return pl.pallas_call(
        _norm_mod_kernel,
        out_shape=jax.ShapeDtypeStruct((n, d), BF16),
        grid=(n // tm,),
        in_specs=[
            pl.BlockSpec((tm, d), lambda i: (i, 0)),
            pl.BlockSpec((1, d), lambda i: (0, 0)),
            pl.BlockSpec((1, d), lambda i: (0, 0)),
        ],
        out_specs=pl.BlockSpec((tm, d), lambda i: (i, 0)),
        compiler_params=_cparams(("parallel",)),
        name="norm_mod",
    )(x, scale, shift)


def _inproj_kernel(h_ref, wg_ref, wu_ref, bg_ref, bu_ref, g_ref, u_ref):
    h = h_ref[...]
    gate = jnp.dot(h, wg_ref[...], preferred_element_type=F32) + bg_ref[...]
    g_ref[...] = jax.nn.gelu(gate, approximate=True).astype(g_ref.dtype)
    u_ref[...] = jnp.dot(h, wu_ref[...], preferred_element_type=F32) + bu_ref[...]


def _inproj(h, w_in, b_in):
    n, d = h.shape
    dl = w_in.shape[1] // 2
    tm = min(1024, n)
    tn = min(512, dl)
    nj = dl // tn
    return pl.pallas_call(
        _inproj_kernel,
        out_shape=(jax.ShapeDtypeStruct((n, dl), BF16), jax.ShapeDtypeStruct((n, dl), F32)),
        grid=(n // tm, nj),
        in_specs=[
            pl.BlockSpec((tm, d), lambda i, j: (i, 0)),
            pl.BlockSpec((d, tn), lambda i, j: (0, j)),
            pl.BlockSpec((d, tn), lambda i, j: (0, j + nj)),
            pl.BlockSpec((1, tn), lambda i, j: (0, j)),
            pl.BlockSpec((1, tn), lambda i, j: (0, j + nj)),
        ],
        out_specs=(pl.BlockSpec((tm, tn), lambda i, j: (i, j)),
                   pl.BlockSpec((tm, tn), lambda i, j: (i, j))),
        compiler_params=_cparams(("parallel", "parallel")),
        name="lru_inproj",
    )(h, w_in, w_in, b_in, b_in)


def _lru_kernel(u_ref, g_ref, cw_ref, cb_ref, gaw_ref, gab_ref, gxw_ref, gxb_ref, lam_ref,
                z_ref, uext, hcar, a_s, b_s, *, T, CB, HD):
    t = pl.program_id(1)

    @pl.when(t == 0)
    def _():
        uext[0:8, :] = jnp.zeros((8, CB), F32)
        hcar[...] = jnp.zeros((8, CB), F32)

    uext[8:T + 8, :] = u_ref[...]
    cw = cw_ref[...]
    xc = (cb_ref[...] + cw[0:1, :] * uext[5:T + 5, :] + cw[1:2, :] * uext[6:T + 6, :]
          + cw[2:3, :] * uext[7:T + 7, :] + cw[3:4, :] * uext[8:T + 8, :])
    uext[0:8, :] = uext[T:T + 8, :]

    xcb = xc.astype(BF16)
    rs, xs = [], []
    for h in range(CB // HD):
        xh = xcb[:, h * HD:(h + 1) * HD]
        rs.append(jnp.dot(xh, gaw_ref[h], preferred_element_type=F32))
        xs.append(jnp.dot(xh, gxw_ref[h], preferred_element_type=F32))
    r = jax.nn.sigmoid(jnp.concatenate(rs, axis=1) + gab_ref[...])
    i = jax.nn.sigmoid(jnp.concatenate(xs, axis=1) + gxb_ref[...])
    lam = lam_ref[...]
    log_sig = jnp.minimum(lam, 0.0) - jnp.log(1.0 + jnp.exp(-jnp.abs(lam)))
    a = jnp.exp((LRU_C * r) * log_sig)
    b = jnp.sqrt(1.0 - a * a) * (i * xc)

    G = T // 8
    A = a.reshape(G, 8, CB)
    B = b.reshape(G, 8, CB)
    row = lax.broadcasted_iota(I32, (G, 8, CB), 1)
    for d in (1, 2, 4):
        keep = row >= d
        B = jnp.where(keep, A * pltpu.roll(B, d, axis=1) + B, B)
        A = jnp.where(keep, A * pltpu.roll(A, d, axis=1), A)
    a_s[...] = A.reshape(T, CB)
    b_s[...] = B.reshape(T, CB)

    def carry_step(gi, hprev):
        r0 = pl.multiple_of(gi * 8, 8)
        hrows = b_s[pl.ds(r0, 8), :] + a_s[pl.ds(r0, 8), :] * hprev
        b_s[pl.ds(r0, 8), :] = hrows
        return jnp.broadcast_to(hrows[7:8, :], (8, CB))

    hcar[...] = lax.fori_loop(0, G, carry_step, hcar[...])
    z_ref[...] = (b_s[...] * g_ref[...].astype(F32)).astype(z_ref.dtype)


def _lru_scan(u, g, conv_w, conv_b, ga_w, ga_b, gx_w, gx_b, lam):
    n, dl = u.shape
    nh, hd, _ = ga_w.shape
    T = min(256, n)
    CB = min(1024, dl)
    hpb = CB // hd
    vec = lambda: pl.BlockSpec((1, CB), lambda c, t: (0, c))
    kern = functools.partial(_lru_kernel, T=T, CB=CB, HD=hd)
    return pl.pallas_call(
        kern,
        out_shape=jax.ShapeDtypeStruct((n, dl), BF16),
        grid=(dl // CB, n // T),
        in_specs=[
            pl.BlockSpec((T, CB), lambda c, t: (t, c)),
            pl.BlockSpec((T, CB), lambda c, t: (t, c)),
            pl.BlockSpec((conv_w.shape[0], CB), lambda c, t: (0, c)),
            vec(),
            pl.BlockSpec((hpb, hd, hd), lambda c, t: (c, 0, 0)),
            vec(),
            pl.BlockSpec((hpb, hd, hd), lambda c, t: (c, 0, 0)),
            vec(),
            vec(),
        ],
        out_specs=pl.BlockSpec((T, CB), lambda c, t: (t, c)),
        scratch_shapes=[
            pltpu.VMEM((T + 8, CB), F32),
            pltpu.VMEM((8, CB), F32),
            pltpu.VMEM((T, CB), F32),
            pltpu.VMEM((T, CB), F32),
        ],
        compiler_params=_cparams(("parallel", "arbitrary")),
        name="lru_scan",
    )(u, g, conv_w, conv_b, ga_w, ga_b, gx_w, gx_b, lam)


def _norm_pack_store(acc, sc_ref, sh_ref, h_ref, nj, tn, d):
    ss = jnp.sum(acc[0] * acc[0], axis=-1, keepdims=True)
    for jj in range(1, nj):
        ss = ss + jnp.sum(acc[jj] * acc[jj], axis=-1, keepdims=True)
    inv = _rms_inv(ss, d)
    half = nj // 2
    for jj in range(half):
        lo_c = slice(jj * tn, (jj + 1) * tn)
        hi_c = slice((jj + half) * tn, (jj + half + 1) * tn)
        lo = (acc[jj] * inv) * (1.0 + sc_ref[:, lo_c]) + sh_ref[:, lo_c]
        hi = (acc[jj + half] * inv) * (1.0 + sc_ref[:, hi_c]) + sh_ref[:, hi_c]
        h_ref[:, lo_c] = _pack_pair(lo, hi)


def _outproj_kernel(z_ref, w_ref, b_ref, x_ref, gate_ref, sc_ref, sh_ref, x1_ref, h_ref, acc,
                    *, NJ, TN, D):
    j = pl.program_id(1)
    y = jnp.dot(z_ref[...], w_ref[...], preferred_element_type=F32) + b_ref[...]
    x1 = x_ref[...] + gate_ref[...] * y
    x1_ref[...] = x1
    acc[j] = x1

    @pl.when(j == NJ - 1)
    def _():
        _norm_pack_store(acc, sc_ref, sh_ref, h_ref, NJ, TN, D)


def _outproj(z, w_out, b_out, x, gate, scale, shift):
    n, dl = z.shape
    d = w_out.shape[1]
    tm = min(512, n)
    tn = min(1024, d // 2)
    nj = d // tn
    kern = functools.partial(_outproj_kernel, NJ=nj, TN=tn, D=d)
    return pl.pallas_call(
        kern,
        out_shape=(jax.ShapeDtypeStruct((n, d), F32), jax.ShapeDtypeStruct((n, d // 2), U32)),
        grid=(n // tm, nj),
        in_specs=[
            pl.BlockSpec((tm, dl), lambda i, j: (i, 0)),
            pl.BlockSpec((dl, tn), lambda i, j: (0, j)),
            pl.BlockSpec((1, tn), lambda i, j: (0, j)),
            pl.BlockSpec((tm, tn), lambda i, j: (i, j)),
            pl.BlockSpec((1, tn), lambda i, j: (0, j)),
            pl.BlockSpec((1, d), lambda i, j: (0, 0)),
            pl.BlockSpec((1, d), lambda i, j: (0, 0)),
        ],
        out_specs=(pl.BlockSpec((tm, tn), lambda i, j: (i, j)),
                   pl.BlockSpec((tm, d // 2), lambda i, j: (i, 0))),
        scratch_shapes=[pltpu.VMEM((nj, tm, tn), F32)],
        compiler_params=_cparams(("parallel", "arbitrary")),
        name="lru_outproj",
    )(z, w_out, b_out, x, gate, scale, shift)


def _pool_kernel(x_ref, sc_ref, sh_ref, pw_ref, pb_ref, ps_ref, gate_ref, sc2_ref, sh2_ref,
                 x3_ref, h_ref, ext, s_a, s_b, ybuf, *, T, D, GD):
    t = pl.program_id(0)
    H = POOL_HIST

    @pl.when(t == 0)
    def _():
        ext[0:H, :] = jnp.zeros((H, D), F32)
        s_a[0:8, :] = jnp.zeros((8, GD), F32)
        s_b[0:8, :] = jnp.zeros((8, GD), F32)

    x = x_ref[...]
    inv = _rms_inv(jnp.sum(x * x, axis=-1, keepdims=True), D)
    ext[H:T + H, :] = (x * inv) * (1.0 + sc_ref[...]) + sh_ref[...]

    pos1 = lax.broadcasted_iota(I32, (T, 1), 0) + (t * T + 1)
    for g, w in enumerate(POOL_WINDOWS):
        c = slice(g * GD, (g + 1) * GD)
        if w == 2:
            win = ext[H:T + H, c] + ext[H - 1:T + H - 1, c]
        else:
            s_a[8:T + H, :] = ext[8:T + H, c] + ext[7:T + H - 1, c]
            if w == 4:
                win = s_a[H:T + H, :] + s_a[H - 2:T + H - 2, :]
            else:
                s_b[8:T + H, :] = s_a[8:T + H, :] + s_a[6:T + H - 2, :]
                if w == 8:
                    win = s_b[H:T + H, :] + s_b[H - 4:T + H - 4, :]
                else:
                    s_a[8:T + H, :] = s_b[8:T + H, :] + s_b[4:T + H - 4, :]
                    win = s_a[H:T + H, :] + s_a[H - 8:T + H - 8, :]
        cnt = jnp.minimum(pos1, w).astype(F32)
        pooled = win / cnt - ext[H:T + H, c]
        yg = jnp.dot(pooled.astype(BF16), pw_ref[g], preferred_element_type=F32) + pb_ref[:, c]
        ybuf[:, c] = yg * ps_ref[:, c]
    ext[0:H, :] = ext[T:T + H, :]

    x3 = x + gate_ref[...] * ybuf[...]
    x3_ref[...] = x3
    inv3 = _rms_inv(jnp.sum(x3 * x3, axis=-1, keepdims=True), D)
    h4 = (x3 * inv3) * (1.0 + sc2_ref[...]) + sh2_ref[...]
    h_ref[...] = _pack_pair(h4[:, :D // 2], h4[:, D // 2:])


def _pool_mixer(x, scale, shift, pool_w, pool_b, pool_scale, gate, scale2, shift2):
    n, d = x.shape
    ng, gd, _ = pool_w.shape
    T = min(256, n)
    vec = lambda: pl.BlockSpec((1, d), lambda t: (0, 0))
    kern = functools.partial(_pool_kernel, T=T, D=d, GD=gd)
    return pl.pallas_call(
        kern,
        out_shape=(jax.ShapeDtypeStruct((n, d), F32), jax.ShapeDtypeStruct((n, d // 2), U32)),
        grid=(n // T,),
        in_specs=[
            pl.BlockSpec((T, d), lambda t: (t, 0)),
            vec(), vec(),
            pl.BlockSpec((ng, gd, gd), lambda t: (0, 0, 0)),
            vec(), vec(), vec(), vec(), vec(),
        ],
        out_specs=(pl.BlockSpec((T, d), lambda t: (t, 0)),
                   pl.BlockSpec((T, d // 2), lambda t: (t, 0))),
        scratch_shapes=[
            pltpu.VMEM((T + POOL_HIST, d), F32),
            pltpu.VMEM((T + POOL_HIST, gd), F32),
            pltpu.VMEM((T + POOL_HIST, gd), F32),
            pltpu.VMEM((T, d), F32),
        ],
        compiler_params=_cparams(("arbitrary",)),
        name="pool_mixer",
    )(x, scale, shift, pool_w, pool_b, pool_scale, gate, scale2, shift2)


def _router_kernel(h_ref, wlo_ref, whi_ref, rb_ref, sel_ref, wf_ref, cnt_ref, *, T, E):
    i = pl.program_id(0)
    NG = N_GROUPS
    GS = E // NG
    lo, hi = _unpack_pair(h_ref[...])
    dn = (((1,), (1,)), ((), ()))
    logits = (lax.dot_general(wlo_ref[...], lo.astype(BF16), dn, preferred_element_type=F32)
              + lax.dot_general(whi_ref[...], hi.astype(BF16), dn, preferred_element_type=F32))
    scores = jax.nn.sigmoid(logits).reshape(NG, GS, T)
    biased = scores + rb_ref[...].reshape(NG, GS, 1)

    sub = lax.broadcasted_iota(I32, (NG, GS, T), 1)
    m1 = jnp.max(biased, axis=1, keepdims=True)
    first = jnp.min(jnp.where(biased == m1, sub, GS), axis=1, keepdims=True)
    m2 = jnp.max(jnp.where(sub == first, -jnp.inf, biased), axis=1, keepdims=True)
    gscore = m1 + m2

    gidx = lax.broadcasted_iota(I32, (NG, 1, T), 0)
    beaten = jnp.zeros((NG, 1, T), F32)
    for j in range(NG):
        sj = gscore[j:j + 1]
        beaten = beaten + jnp.where(sj > gscore, 1.0,
                                    jnp.where(sj == gscore, jnp.where(gidx > j, 1.0, 0.0), 0.0))
    masked = jnp.where(beaten < TOPK_GROUPS, biased, -jnp.inf)

    eidx = lax.broadcasted_iota(I32, (NG, GS, T), 0) * GS + sub
    beaten = jnp.zeros((NG, GS, T), F32)
    for g in range(NG):
        mg = masked[g]
        for s in range(GS):
            v = mg[s:s + 1, :][None]
            later = jnp.where(eidx > g * GS + s, 1.0, 0.0)
            beaten = beaten + jnp.where(v > masked, 1.0, jnp.where(v == masked, later, 0.0))
    sel = jnp.where(beaten < TOP_K, 1.0, 0.0)

    picked = sel * scores
    tot = jnp.sum(jnp.sum(picked, axis=1, keepdims=True), axis=0, keepdims=True)
    wf = picked / tot * ROUTED_SCALE

    sel2 = sel.reshape(E, T)
    sel_ref[...] = sel2
    wf_ref[...] = wf.reshape(E, T)

    @pl.when(i == 0)
    def _():
        cnt_ref[...] = jnp.zeros_like(cnt_ref)
    part = sel2[:, 0:128]
    for c in range(1, T // 128):
        part = part + sel2[:, c * 128:(c + 1) * 128]
    cnt_ref[...] += part


def _router(h2, wlo, whi, rbias):
    n, d2 = h2.shape
    e = wlo.shape[0]
    T = min(512, n)
    kern = functools.partial(_router_kernel, T=T, E=e)
    return pl.pallas_call(
        kern,
        out_shape=(jax.ShapeDtypeStruct((e, n), F32), jax.ShapeDtypeStruct((e, n), F32),
                   jax.ShapeDtypeStruct((e, 128), F32)),
        grid=(n // T,),
        in_specs=[
            pl.BlockSpec((T, d2), lambda i: (i, 0)),
            pl.BlockSpec((e, d2), lambda i: (0, 0)),
            pl.BlockSpec((e, d2), lambda i: (0, 0)),
            pl.BlockSpec((e, 1), lambda i: (0, 0)),
        ],
        out_specs=(pl.BlockSpec((e, T), lambda i: (0, i)),
                   pl.BlockSpec((e, T), lambda i: (0, i)),
                   pl.BlockSpec((e, 128), lambda i: (0, 0))),
        compiler_params=_cparams(("arbitrary",)),
        name="moe_router",
    )(h2, wlo, whi, rbias)


def _rank_kernel(sel_ref, wf_ref, ps_ref, dest_ref, wk_ref, carry, *, T, E):
    i = pl.program_id(0)

    @pl.when(i == 0)
    def _():
        carry[...] = jnp.zeros_like(carry)

    sel = sel_ref[...]
    selb = sel.astype(BF16)
    before = jnp.where(lax.broadcasted_iota(I32, (T, T), 0) < lax.broadcasted_iota(I32, (T, T), 1),
                       1.0, 0.0).astype(BF16)
    rank = jnp.dot(selb, before, preferred_element_type=F32)
    destf = ps_ref[...] + carry[...] + rank
    lower = jnp.where(lax.broadcasted_iota(I32, (E, E), 1) < lax.broadcasted_iota(I32, (E, E), 0),
                      1.0, 0.0).astype(BF16)
    slot = jnp.dot(lower, selb, preferred_element_type=F32)
    wf = wf_ref[...]
    for k in range(TOP_K):
        mk = jnp.where(slot == k, sel, 0.0)
        dest_ref[k:k + 1, :] = jnp.sum(mk * destf, axis=0, keepdims=True).astype(I32)
        wk_ref[k:k + 1, :] = jnp.sum(mk * wf, axis=0, keepdims=True)
    carry[...] += jnp.sum(sel, axis=1, keepdims=True)


def _rank(sel, wf, pad_start):
    e, n = sel.shape
    T = min(512, n)
    kern = functools.partial(_rank_kernel, T=T, E=e)
    return pl.pallas_call(
        kern,
        out_shape=(jax.ShapeDtypeStruct((TOP_K, n), I32), jax.ShapeDtypeStruct((TOP_K, n), F32)),
        grid=(n // T,),
        in_specs=[
            pl.BlockSpec((e, T), lambda i: (0, i)),
            pl.BlockSpec((e, T), lambda i: (0, i)),
            pl.BlockSpec((e, 1), lambda i: (0, 0)),
        ],
        out_specs=(pl.BlockSpec((TOP_K, T), lambda i: (0, i)),
                   pl.BlockSpec((TOP_K, T), lambda i: (0, i))),
        scratch_shapes=[pltpu.VMEM((e, 1), F32)],
        compiler_params=_cparams(("arbitrary",)),
        name="moe_rank",
    )(sel, wf, pad_start)


def _dispatch_kernel(zs_ref, zl_ref, dest_ref, h_ref, xs_ref, zbuf, sem, zsem, *, T, E):
    i = pl.program_id(0)

    def row_copy(t, k):
        return pltpu.make_async_copy(h_ref.at[pl.ds(t, 1), :],
                                     xs_ref.at[pl.ds(dest_ref[k, t], 1), :], sem)

    def for_each_pad_row(fn):
        def per_expert(e, _):
            def per_row(r, _):
                fn(pltpu.make_async_copy(zbuf.at[pl.ds(0, 1), :],
                                         xs_ref.at[pl.ds(zs_ref[e] + r, 1), :], zsem))
                return 0
            lax.fori_loop(0, zl_ref[e], per_row, 0)
            return 0
        lax.fori_loop(0, E, per_expert, 0)

    @pl.when(i == 0)
    def _():
        zbuf[...] = jnp.zeros_like(zbuf)
        for_each_pad_row(lambda cp: cp.start())

    def issue(t, _):
        for k in range(TOP_K):
            row_copy(t, k).start()
        return 0
    lax.fori_loop(0, T, issue, 0)

    def drain(t, _):
        for k in range(TOP_K):
            row_copy(t, k).wait()
        return 0
    lax.fori_loop(0, T, drain, 0)

    @pl.when(i == 0)
    def _():
        for_each_pad_row(lambda cp: cp.wait())


def _dispatch(h2, dest, zero_start, zero_len, n_rows):
    n, d2 = h2.shape
    e = zero_start.shape[0]
    T = min(256, n)
    kern = functools.partial(_dispatch_kernel, T=T, E=e)
    grid_spec = pltpu.PrefetchScalarGridSpec(
        num_scalar_prefetch=2,
        grid=(n // T,),
        in_specs=[
            pl.BlockSpec((TOP_K, T), lambda i, zs, zl: (0, i), memory_space=pltpu.SMEM),
            pl.BlockSpec((T, d2), lambda i, zs, zl: (i, 0)),
        ],
        out_specs=pl.BlockSpec(memory_space=pl.ANY),
        scratch_shapes=[
            pltpu.VMEM((8, d2), U32),
            pltpu.SemaphoreType.DMA(()),
            pltpu.SemaphoreType.DMA(()),
        ],
    )
    return pl.pallas_call(
        kern,
        out_shape=jax.ShapeDtypeStruct((n_rows, d2), U32),
        grid_spec=grid_spec,
        compiler_params=_cparams(("arbitrary",)),
        name="moe_dispatch",
    )(zero_start, zero_len, dest, h2)


def _ffn_kernel(be_ref, na_ref, x_ref, wgu_ref, wd_ref, y_ref, *, F, D2):
    b = pl.program_id(0)

    @pl.when(b < na_ref[0])
    def _():
        lo, hi = _unpack_pair(x_ref[...])
        gu = (jnp.dot(lo.astype(BF16), wgu_ref[0:D2, :], preferred_element_type=F32)
              + jnp.dot(hi.astype(BF16), wgu_ref[D2:2 * D2, :], preferred_element_type=F32))
        act = jax.nn.silu(gu[:, :F]) * gu[:, F:]
        y = jnp.dot(act.astype(BF16), wd_ref[...], preferred_element_type=F32)
        y_ref[...] = _pack_pair(y[:, :D2], y[:, D2:])


def _grouped_ffn(xs, block_expert, n_active, wgu, wd):
    rows, d2 = xs.shape
    e, d, f2 = wgu.shape
    f = f2 // 2
    nb = rows // EXPERT_BLOCK
    kern = functools.partial(_ffn_kernel, F=f, D2=d2)

    def blk(b, be, na):
        return jnp.minimum(b, na[0] - 1)

    grid_spec = pltpu.PrefetchScalarGridSpec(
        num_scalar_prefetch=2,
        grid=(nb,),
        in_specs=[
            pl.BlockSpec((EXPERT_BLOCK, d2), lambda b, be, na: (blk(b, be, na), 0)),
            pl.BlockSpec((None, d, f2), lambda b, be, na: (be[blk(b, be, na)], 0, 0)),
            pl.BlockSpec((None, f, d), lambda b, be, na: (be[blk(b, be, na)], 0, 0)),
        ],
        out_specs=pl.BlockSpec((EXPERT_BLOCK, d2), lambda b, be, na: (blk(b, be, na), 0)),
    )
    return pl.pallas_call(
        kern,
        out_shape=jax.ShapeDtypeStruct((rows, d2), U32),
        grid_spec=grid_spec,
        compiler_params=_cparams(("arbitrary",)),
        name="moe_ffn",
    )(block_expert, n_active, xs, wgu, wd)


def _combine_kernel(dest_ref, wk_ref, ys_ref, ysh_ref, x_ref, gate_ref, gain_ref, o_ref, buf, sem,
                    *, T, D, FINAL):
    def row_copy(t, k):
        return pltpu.make_async_copy(ys_ref.at[pl.ds(dest_ref[k, t], 1), :],
                                     buf.at[k, pl.ds(t, 1), :], sem)

    def issue(t, _):
        for k in range(TOP_K):
            row_copy(t, k).start()
        return 0
    lax.fori_loop(0, T, issue, 0)

    def drain(t, _):
        for k in range(TOP_K):
            row_copy(t, k).wait()
        return 0
    lax.fori_loop(0, T, drain, 0)

    routed_lo = jnp.zeros((T, D // 2), F32)
    routed_hi = jnp.zeros((T, D // 2), F32)
    for k in range(TOP_K):
        lo, hi = _unpack_pair(buf[k])
        w = wk_ref[:, k:k + 1]
        routed_lo = routed_lo + w * lo
        routed_hi = routed_hi + w * hi
    sh_lo, sh_hi = _unpack_pair(ysh_ref[...])
    gate = gate_ref[...]
    x = x_ref[...]
    xo_lo = x[:, :D // 2] + gate[:, :D // 2] * (routed_lo + sh_lo)
    xo_hi = x[:, D // 2:] + gate[:, D // 2:] * (routed_hi + sh_hi)
    if FINAL:
        ss = (jnp.sum(xo_lo * xo_lo, axis=-1, keepdims=True)
              + jnp.sum(xo_hi * xo_hi, axis=-1, keepdims=True))
        inv = _rms_inv(ss, D)
        gain = gain_ref[...]
        xo_lo = (xo_lo * inv) * gain[:, :D // 2]
        xo_hi = (xo_hi * inv) * gain[:, D // 2:]
    o_ref[:, :D // 2] = xo_lo
    o_ref[:, D // 2:] = xo_hi


def _combine(dest, wk_t, ys, ysh, x, gate, gain, final):
    n, d = x.shape
    d2 = d // 2
    T = min(128, n)
    kern = functools.partial(_combine_kernel, T=T, D=d, FINAL=final)
    return pl.pallas_call(
        kern,
        out_shape=jax.ShapeDtypeStruct((n, d), F32),
        grid=(n // T,),
        in_specs=[
            pl.BlockSpec((TOP_K, T), lambda i: (0, i), memory_space=pltpu.SMEM),
            pl.BlockSpec((T, TOP_K), lambda i: (i, 0)),
            pl.BlockSpec(memory_space=pl.ANY),
            pl.BlockSpec((T, d2), lambda i: (i, 0)),
            pl.BlockSpec((T, d), lambda i: (i, 0)),
            pl.BlockSpec((1, d), lambda i: (0, 0)),
            pl.BlockSpec((1, d), lambda i: (0, 0)),
        ],
        out_specs=pl.BlockSpec((T, d), lambda i: (i, 0)),
        scratch_shapes=[
            pltpu.VMEM((TOP_K, T, d2), U32),
            pltpu.SemaphoreType.DMA(()),
        ],
        compiler_params=_cparams(("arbitrary",)),
        name="moe_combine",
    )(dest, wk_t, ys, ysh, x, gate, gain)


def _moe(h2, x, gate, gain, final, router_w, router_bias, w_gate, w_up, w_down,
         ws_gate, ws_up, ws_down):
    n, d2 = h2.shape
    e = router_w.shape[1]
    blk = EXPERT_BLOCK

    rw_t = router_w.T.astype(BF16)
    sel, wf, cnt = _router(h2, rw_t[:, :d2], rw_t[:, d2:], router_bias.reshape(e, 1))

    counts = jnp.sum(cnt, axis=1).astype(I32)
    padded = (counts + blk - 1) // blk * blk
    pad_end = jnp.cumsum(padded)
    pad_start = pad_end - padded
    n_blocks = (n * TOP_K + e * (blk - 1) + blk - 1) // blk
    block_start = jnp.arange(n_blocks, dtype=I32) * blk
    block_expert = jnp.minimum(
        jnp.sum((pad_end[None, :] <= block_start[:, None]).astype(I32), axis=1), e - 1)
    n_active = (pad_end[-1:] // blk).astype(I32)

    dest, wk = _rank(sel, wf, pad_start.astype(F32).reshape(e, 1))
    xs = _dispatch(h2, dest, pad_start + counts, padded - counts, n_blocks * blk)

    wgu = jnp.concatenate([w_gate, w_up], axis=-1).astype(BF16)
    ys = _grouped_ffn(xs, block_expert, n_active, wgu, w_down.astype(BF16))

    wsgu = jnp.concatenate([ws_gate, ws_up], axis=-1).astype(BF16)[None]
    ysh = _grouped_ffn(h2, jnp.zeros((n // blk,), I32), jnp.full((1,), n // blk, I32),
                       wsgu, ws_down.astype(BF16)[None])

    return _combine(dest, wk.T, ys, ysh, x, gate, gain, final)


def kernel(x, c, mod_w, mod_b, lru_w_in, lru_b_in, lru_conv_w, lru_conv_b, lru_gate_a_w,
           lru_gate_a_b, lru_gate_x_w, lru_gate_x_b, lru_lambda, lru_w_out, lru_b_out, pool_w,
           pool_b, pool_scale, router_w, router_bias, expert_w_gate, expert_w_up, expert_w_down,
           shared_w_gate, shared_w_up, shared_w_down, final_gain):
    bsz, seq, d = x.shape
    depth = mod_w.shape[0]
    assert bsz == 1, "one sequence per call"
    xf = x.reshape(seq, d)
    row = lambda v: v.reshape(1, -1)

    mod = _adaln_all(c, mod_w, mod_b)

    def adaln(i, s):
        m = mod[2 * i + s]
        return m[:, :d], m[:, d:2 * d], m[:, 2 * d:]

    gain = row(final_gain)
    for i in range(depth):
        j = i // 2
        shift, scale, gate = adaln(i, 0)
        shift2, scale2, gate2 = adaln(i, 1)
        if i % 2 == 0:
            h = _norm_mod(xf, scale, shift)
            g, u = _inproj(h, lru_w_in[j].astype(BF16), row(lru_b_in[j]))
            z = _lru_scan(u, g, lru_conv_w[j], row(lru_conv_b[j]),
                          lru_gate_a_w[j].astype(BF16), row(lru_gate_a_b[j]),
                          lru_gate_x_w[j].astype(BF16), row(lru_gate_x_b[j]), row(lru_lambda[j]))
            xf, h2 = _outproj(z, lru_w_out[j].astype(BF16), row(lru_b_out[j]), xf, gate,
                              scale2, shift2)
        else:
            xf, h2 = _pool_mixer(xf, scale, shift, pool_w[j].astype(BF16), row(pool_b[j]),
                                 row(pool_scale[j]), gate, scale2, shift2)
        xf = _moe(h2, xf, gate2, gain, i == depth - 1, router_w[i], router_bias[i],
                  expert_w_gate[i], expert_w_up[i], expert_w_down[i],
                  shared_w_gate[i], shared_w_up[i], shared_w_down[i])
    return xf.reshape(bsz, seq, d)
```

```python
import functools

import jax
import jax.numpy as jnp
from jax import lax
from jax.experimental import pallas as pl
from jax.experimental.pallas import tpu as pltpu

F32 = jnp.float32
BF16 = jnp.bfloat16
U32 = jnp.uint32
I32 = jnp.int32

EPS = 1e-6
LRU_C = 8.0
N_GROUPS = 8
TOPK_GROUPS = 4
TOP_K = 8
ROUTED_SCALE = 2.5
POOL_WINDOWS = (2, 4, 8, 16)
EXPERT_BLOCK = 512
FFN_CHAINS = 4
POOL_HIST = 24
VMEM_LIMIT = 56 * 1024 * 1024


def _cparams(sem, vmem=VMEM_LIMIT):
    return pltpu.CompilerParams(dimension_semantics=sem, vmem_limit_bytes=vmem)


def _pack_pair(lo, hi):
    lo_b = lax.bitcast_convert_type(lo.astype(BF16).astype(F32), U32)
    hi_b = lax.bitcast_convert_type(hi.astype(BF16).astype(F32), U32)
    return hi_b | (lo_b >> 16)


def _unpack_pair(w):
    lo = lax.bitcast_convert_type(w << 16, F32)
    hi = lax.bitcast_convert_type(w & jnp.uint32(0xFFFF0000), F32)
    return lo, hi


def _rms_inv(ss, d):
    return lax.rsqrt(ss / d + EPS)


def _mod_kernel(c_ref, w_ref, b_ref, o_ref):
    c = c_ref[...]
    s = c * jax.nn.sigmoid(c)
    s8 = jnp.broadcast_to(s, (8, s.shape[1])).astype(BF16)
    acc = jnp.dot(s8, w_ref[...].astype(BF16), preferred_element_type=F32)
    o_ref[...] = acc[0:1, :] + b_ref[...]


def _adaln_all(c, mod_w, mod_b):
    depth, two, d, d3 = mod_w.shape
    na = depth * two
    tn = 512
    w = mod_w.reshape(na, d, d3)
    b = mod_b.reshape(na, 1, d3)
    return pl.pallas_call(
        _mod_kernel,
        out_shape=jax.ShapeDtypeStruct((na, 1, d3), F32),
        grid=(na, d3 // tn),
        in_specs=[
            pl.BlockSpec((1, d), lambda a, j: (0, 0)),
            pl.BlockSpec((None, d, tn), lambda a, j: (a, 0, j)),
            pl.BlockSpec((None, 1, tn), lambda a, j: (a, 0, j)),
        ],
        out_specs=pl.BlockSpec((None, 1, tn), lambda a, j: (a, 0, j)),
        compiler_params=_cparams(("arbitrary", "arbitrary")),
        name="adaln_mod",
    )(c, w, b)


def _norm_mod_kernel(x_ref, sc_ref, sh_ref, o_ref):
    x = x_ref[...]
    inv = _rms_inv(jnp.sum(x * x, axis=-1, keepdims=True), x.shape[-1])
    o_ref[...] = ((x * inv) * (1.0 + sc_ref[...]) + sh_ref[...]).astype(o_ref.dtype)


def _norm_mod(x, scale, shift):
    n, d = x.shape
    tm = min(512, n)
    return pl.pallas_call(
        _norm_mod_kernel,
        out_shape=jax.ShapeDtypeStruct((n, d), BF16),
        grid=(n // tm,),
        in_specs=[
            pl.BlockSpec((tm, d), lambda i: (i, 0)),
            pl.BlockSpec((1, d), lambda i: (0, 0)),
            pl.BlockSpec((1, d), lambda i: (0, 0)),
        ],
        out_specs=pl.BlockSpec((tm, d), lambda i: (i, 0)),
        compiler_params=_cparams(("arbitrary",)),
        name="norm_mod",
    )(x, scale, shift)


def _inproj_kernel(h_ref, hprev_ref, wg_ref, wu_ref, bg_ref, bu_ref, cw_ref, cb_ref, g_ref, xc_ref,
                   uext, *, TM):
    i = pl.program_id(0)
    h = h_ref[...]
    uprev = jnp.dot(hprev_ref[...], wu_ref[...], preferred_element_type=F32) + bu_ref[...]
    uext[0:8, :] = jnp.where(i > 0, uprev[8:16, :], 0.0)
    uext[8:TM + 8, :] = jnp.dot(h, wu_ref[...], preferred_element_type=F32) + bu_ref[...]
    gate = jnp.dot(h, wg_ref[...], preferred_element_type=F32) + bg_ref[...]
    cw = cw_ref[...]
    xc_ref[...] = (cb_ref[...] + cw[0:1, :] * uext[5:TM + 5, :] + cw[1:2, :] * uext[6:TM + 6, :]
                   + cw[2:3, :] * uext[7:TM + 7, :] + cw[3:4, :] * uext[8:TM + 8, :])
    g_ref[...] = jax.nn.gelu(gate, approximate=True).astype(g_ref.dtype)


def _inproj(h, w_in, b_in, conv_w, conv_b):
    n, d = h.shape
    dl = w_in.shape[1] // 2
    tm = min(1024, n)
    tn = min(512, dl)
    nj = dl // tn
    kern = functools.partial(_inproj_kernel, TM=tm)
    return pl.pallas_call(
        kern,
        out_shape=(jax.ShapeDtypeStruct((n, dl), BF16), jax.ShapeDtypeStruct((n, dl), F32)),
        grid=(n // tm, nj),
        in_specs=[
            pl.BlockSpec((tm, d), lambda i, j: (i, 0)),
            pl.BlockSpec((16, d), lambda i, j: (jnp.maximum(i * (tm // 16) - 1, 0), 0)),
            pl.BlockSpec((d, tn), lambda i, j: (0, j)),
            pl.BlockSpec((d, tn), lambda i, j: (0, j + nj)),
            pl.BlockSpec((1, tn), lambda i, j: (0, j)),
            pl.BlockSpec((1, tn), lambda i, j: (0, j + nj)),
            pl.BlockSpec((conv_w.shape[0], tn), lambda i, j: (0, j)),
            pl.BlockSpec((1, tn), lambda i, j: (0, j)),
        ],
        out_specs=(pl.BlockSpec((tm, tn), lambda i, j: (i, j)),
                   pl.BlockSpec((tm, tn), lambda i, j: (i, j))),
        scratch_shapes=[pltpu.VMEM((tm + 8, tn), F32)],
        compiler_params=_cparams(("arbitrary", "arbitrary")),
        name="lru_inproj",
    )(h, h, w_in, w_in, b_in, b_in, conv_w, conv_b)


def _lru_kernel(xc_ref, g_ref, gaw_ref, gab_ref, gxw_ref, gxb_ref, lam_ref,
                z_ref, hcar, a_s, b_s, *, T, CB, HD):
    t = pl.program_id(1)

    @pl.when(t == 0)
    def _():
        hcar[...] = jnp.zeros((8, CB), F32)

    xc = xc_ref[...]
    xcb = xc.astype(BF16)
    rs, xs = [], []
    for h in range(CB // HD):
        xh = xcb[:, h * HD:(h + 1) * HD]
        rs.append(jnp.dot(xh, gaw_ref[h], preferred_element_type=F32))
        xs.append(jnp.dot(xh, gxw_ref[h], preferred_element_type=F32))
    r = jax.nn.sigmoid(jnp.concatenate(rs, axis=1) + gab_ref[...])
    i = jax.nn.sigmoid(jnp.concatenate(xs, axis=1) + gxb_ref[...])
    lam = lam_ref[...]
    log_sig = jnp.minimum(lam, 0.0) - jnp.log(1.0 + jnp.exp(-jnp.abs(lam)))
    a = jnp.exp((LRU_C * r) * log_sig)
    b = jnp.sqrt(1.0 - a * a) * (i * xc)

    G = T // 8
    A = a.reshape(G, 8, CB)
    B = b.reshape(G, 8, CB)
    row = lax.broadcasted_iota(I32, (G, 8, CB), 1)
    for d in (1, 2, 4):
        keep = row >= d
        B = jnp.where(keep, A * pltpu.roll(B, d, axis=1) + B, B)
        A = jnp.where(keep, A * pltpu.roll(A, d, axis=1), A)
    a_s[...] = A.reshape(T, CB)
    b_s[...] = B.reshape(T, CB)

    def carry_step(gi, hprev):
        r0 = pl.multiple_of(gi * 8, 8)
        hrows = b_s[pl.ds(r0, 8), :] + a_s[pl.ds(r0, 8), :] * hprev
        b_s[pl.ds(r0, 8), :] = hrows
        return jnp.broadcast_to(hrows[7:8, :], (8, CB))

    hcar[...] = lax.fori_loop(0, G, carry_step, hcar[...])
    z_ref[...] = (b_s[...] * g_ref[...].astype(F32)).astype(z_ref.dtype)


def _lru_scan(xc, g, ga_w, ga_b, gx_w, gx_b, lam):
    n, dl = xc.shape
    nh, hd, _ = ga_w.shape
    T = min(256, n)
    CB = min(1024, dl)
    hpb = CB // hd
    vec = lambda: pl.BlockSpec((1, CB), lambda c, t: (0, c))
    kern = functools.partial(_lru_kernel, T=T, CB=CB, HD=hd)
    return pl.pallas_call(
        kern,
        out_shape=jax.ShapeDtypeStruct((n, dl), BF16),
        grid=(dl // CB, n // T),
        in_specs=[
            pl.BlockSpec((T, CB), lambda c, t: (t, c)),
            pl.BlockSpec((T, CB), lambda c, t: (t, c)),
            pl.BlockSpec((hpb, hd, hd), lambda c, t: (c, 0, 0)),
            vec(),
            pl.BlockSpec((hpb, hd, hd), lambda c, t: (c, 0, 0)),
            vec(),
            vec(),
        ],
        out_specs=pl.BlockSpec((T, CB), lambda c, t: (t, c)),
        scratch_shapes=[
            pltpu.VMEM((8, CB), F32),
            pltpu.VMEM((T, CB), F32),
            pltpu.VMEM((T, CB), F32),
        ],
        compiler_params=_cparams(("arbitrary", "arbitrary")),
        name="lru_scan",
    )(xc, g, ga_w, ga_b, gx_w, gx_b, lam)


def _norm_pack_store(acc, sc_ref, sh_ref, h_ref, nj, tn, d):
    ss = jnp.sum(acc[0] * acc[0], axis=-1, keepdims=True)
    for jj in range(1, nj):
        ss = ss + jnp.sum(acc[jj] * acc[jj], axis=-1, keepdims=True)
    inv = _rms_inv(ss, d)
    half = nj // 2
    for jj in range(half):
        lo_c = slice(jj * tn, (jj + 1) * tn)
        hi_c = slice((jj + half) * tn, (jj + half + 1) * tn)
        lo = (acc[jj] * inv) * (1.0 + sc_ref[:, lo_c]) + sh_ref[:, lo_c]
        hi = (acc[jj + half] * inv) * (1.0 + sc_ref[:, hi_c]) + sh_ref[:, hi_c]
        h_ref[:, lo_c] = _pack_pair(lo, hi)


def _outproj_kernel(z_ref, w_ref, b_ref, x_ref, gate_ref, sc_ref, sh_ref, x1_ref, h_ref, acc,
                    *, NJ, TN, D):
    j = pl.program_id(1)
    y = jnp.dot(z_ref[...], w_ref[...], preferred_element_type=F32) + b_ref[...]
    x1 = x_ref[...] + gate_ref[...] * y
    x1_ref[...] = x1
    acc[j] = x1

    @pl.when(j == NJ - 1)
    def _():
        _norm_pack_store(acc, sc_ref, sh_ref, h_ref, NJ, TN, D)


def _outproj(z, w_out, b_out, x, gate, scale, shift):
    n, dl = z.shape
    d = w_out.shape[1]
    tm = min(512, n)
    tn = min(1024, d // 2)
    nj = d // tn
    kern = functools.partial(_outproj_kernel, NJ=nj, TN=tn, D=d)
    return pl.pallas_call(
        kern,
        out_shape=(jax.ShapeDtypeStruct((n, d), F32), jax.ShapeDtypeStruct((n, d // 2), U32)),
        grid=(n // tm, nj),
        in_specs=[
            pl.BlockSpec((tm, dl), lambda i, j: (i, 0)),
            pl.BlockSpec((dl, tn), lambda i, j: (0, j)),
            pl.BlockSpec((1, tn), lambda i, j: (0, j)),
            pl.BlockSpec((tm, tn), lambda i, j: (i, j)),
            pl.BlockSpec((1, tn), lambda i, j: (0, j)),
            pl.BlockSpec((1, d), lambda i, j: (0, 0)),
            pl.BlockSpec((1, d), lambda i, j: (0, 0)),
        ],
        out_specs=(pl.BlockSpec((tm, tn), lambda i, j: (i, j)),
                   pl.BlockSpec((tm, d // 2), lambda i, j: (i, 0))),
        scratch_shapes=[pltpu.VMEM((nj, tm, tn), F32)],
        compiler_params=_cparams(("arbitrary", "arbitrary")),
        name="lru_outproj",
    )(z, w_out, b_out, x, gate, scale, shift)


def _pool_kernel(x_ref, sc_ref, sh_ref, pw_ref, pb_ref, ps_ref, gate_ref, sc2_ref, sh2_ref,
                 x3_ref, h_ref, ext, s_a, s_b, ybuf, *, T, D, GD):
    t = pl.program_id(0)
    H = POOL_HIST

    @pl.when(t == 0)
    def _():
        ext[0:H, :] = jnp.zeros((H, D), F32)
        s_a[0:8, :] = jnp.zeros((8, GD), F32)
        s_b[0:8, :] = jnp.zeros((8, GD), F32)

    x = x_ref[...]
    inv = _rms_inv(jnp.sum(x * x, axis=-1, keepdims=True), D)
    ext[H:T + H, :] = (x * inv) * (1.0 + sc_ref[...]) + sh_ref[...]

    pos1 = lax.broadcasted_iota(I32, (T, 1), 0) + (t * T + 1)
    for g, w in enumerate(POOL_WINDOWS):
        c = slice(g * GD, (g + 1) * GD)
        if w == 2:
            win = ext[H:T + H, c] + ext[H - 1:T + H - 1, c]
        else:
            s_a[8:T + H, :] = ext[8:T + H, c] + ext[7:T + H - 1, c]
            if w == 4:
                win = s_a[H:T + H, :] + s_a[H - 2:T + H - 2, :]
            else:
                s_b[8:T + H, :] = s_a[8:T + H, :] + s_a[6:T + H - 2, :]
                if w == 8:
                    win = s_b[H:T + H, :] + s_b[H - 4:T + H - 4, :]
                else:
                    s_a[8:T + H, :] = s_b[8:T + H, :] + s_b[4:T + H - 4, :]
                    win = s_a[H:T + H, :] + s_a[H - 8:T + H - 8, :]
        cnt = jnp.minimum(pos1, w).astype(F32)
        pooled = win / cnt - ext[H:T + H, c]
        yg = jnp.dot(pooled.astype(BF16), pw_ref[g], preferred_element_type=F32) + pb_ref[:, c]
        ybuf[:, c] = yg * ps_ref[:, c]
    ext[0:H, :] = ext[T:T + H, :]

    x3 = x + gate_ref[...] * ybuf[...]
    x3_ref[...] = x3
    inv3 = _rms_inv(jnp.sum(x3 * x3, axis=-1, keepdims=True), D)
    h4 = (x3 * inv3) * (1.0 + sc2_ref[...]) + sh2_ref[...]
    h_ref[...] = _pack_pair(h4[:, :D // 2], h4[:, D // 2:])


def _pool_mixer(x, scale, shift, pool_w, pool_b, pool_scale, gate, scale2, shift2):
    n, d = x.shape
    ng, gd, _ = pool_w.shape
    T = min(256, n)
    vec = lambda: pl.BlockSpec((1, d), lambda t: (0, 0))
    kern = functools.partial(_pool_kernel, T=T, D=d, GD=gd)
    return pl.pallas_call(
        kern,
        out_shape=(jax.ShapeDtypeStruct((n, d), F32), jax.ShapeDtypeStruct((n, d // 2), U32)),
        grid=(n // T,),
        in_specs=[
            pl.BlockSpec((T, d), lambda t: (t, 0)),
            vec(), vec(),
            pl.BlockSpec((ng, gd, gd), lambda t: (0, 0, 0)),
            vec(), vec(), vec(), vec(), vec(),
        ],
        out_specs=(pl.BlockSpec((T, d), lambda t: (t, 0)),
                   pl.BlockSpec((T, d // 2), lambda t: (t, 0))),
        scratch_shapes=[
            pltpu.VMEM((T + POOL_HIST, d), F32),
            pltpu.VMEM((T + POOL_HIST, gd), F32),
            pltpu.VMEM((T + POOL_HIST, gd), F32),
            pltpu.VMEM((T, d), F32),
        ],
        compiler_params=_cparams(("arbitrary",)),
        name="pool_mixer",
    )(x, scale, shift, pool_w, pool_b, pool_scale, gate, scale2, shift2)


def _router_kernel(h_ref, wlo_ref, whi_ref, rb_ref, sel_ref, wf_ref, cnt_ref, *, T, E):
    i = pl.program_id(0)
    NG = N_GROUPS
    GS = E // NG
    lo, hi = _unpack_pair(h_ref[...])
    dn = (((1,), (1,)), ((), ()))
    logits = (lax.dot_general(wlo_ref[...], lo.astype(BF16), dn, preferred_element_type=F32)
              + lax.dot_general(whi_ref[...], hi.astype(BF16), dn, preferred_element_type=F32))
    scores = jax.nn.sigmoid(logits).reshape(NG, GS, T)
    biased = scores + rb_ref[...].reshape(NG, GS, 1)

    sub = lax.broadcasted_iota(I32, (NG, GS, T), 1)
    m1 = jnp.max(biased, axis=1, keepdims=True)
    first = jnp.min(jnp.where(biased == m1, sub, GS), axis=1, keepdims=True)
    m2 = jnp.max(jnp.where(sub == first, -jnp.inf, biased), axis=1, keepdims=True)
    gscore = m1 + m2

    gidx = lax.broadcasted_iota(I32, (NG, 1, T), 0)
    beaten = jnp.zeros((NG, 1, T), F32)
    for j in range(NG):
        sj = gscore[j:j + 1]
        beaten = beaten + jnp.where(sj > gscore, 1.0,
                                    jnp.where(sj == gscore, jnp.where(gidx > j, 1.0, 0.0), 0.0))
    masked = jnp.where(beaten < TOPK_GROUPS, biased, -jnp.inf)

    eidx = lax.broadcasted_iota(I32, (NG, GS, T), 0) * GS + sub
    beaten = jnp.zeros((NG, GS, T), F32)
    for g in range(NG):
        mg = masked[g]
        for s in range(GS):
            v = mg[s:s + 1, :][None]
            later = jnp.where(eidx > g * GS + s, 1.0, 0.0)
            beaten = beaten + jnp.where(v > masked, 1.0, jnp.where(v == masked, later, 0.0))
    sel = jnp.where(beaten < TOP_K, 1.0, 0.0)

    picked = sel * scores
    tot = jnp.sum(jnp.sum(picked, axis=1, keepdims=True), axis=0, keepdims=True)
    wf = picked / tot * ROUTED_SCALE

    sel2 = sel.reshape(E, T)
    sel_ref[...] = sel2
    wf_ref[...] = wf.reshape(E, T)

    @pl.when(i == 0)
    def _():
        cnt_ref[...] = jnp.zeros_like(cnt_ref)
    part = sel2[:, 0:128]
    for c in range(1, T // 128):
        part = part + sel2[:, c * 128:(c + 1) * 128]
    cnt_ref[...] += part


def _router(h2, wlo, whi, rbias):
    n, d2 = h2.shape
    e = wlo.shape[0]
    T = min(512, n)
    kern = functools.partial(_router_kernel, T=T, E=e)
    return pl.pallas_call(
        kern,
        out_shape=(jax.ShapeDtypeStruct((e, n), F32), jax.ShapeDtypeStruct((e, n), F32),
                   jax.ShapeDtypeStruct((e, 128), F32)),
        grid=(n // T,),
        in_specs=[
            pl.BlockSpec((T, d2), lambda i: (i, 0)),
            pl.BlockSpec((e, d2), lambda i: (0, 0)),
            pl.BlockSpec((e, d2), lambda i: (0, 0)),
            pl.BlockSpec((e, 1), lambda i: (0, 0)),
        ],
        out_specs=(pl.BlockSpec((e, T), lambda i: (0, i)),
                   pl.BlockSpec((e, T), lambda i: (0, i)),
                   pl.BlockSpec((e, 128), lambda i: (0, 0))),
        compiler_params=_cparams(("arbitrary",)),
        name="moe_router",
    )(h2, wlo, whi, rbias)


def _rank_kernel(sel_ref, wf_ref, ps_ref, dest_ref, wk_ref, carry, *, T, E):
    i = pl.program_id(0)

    @pl.when(i == 0)
    def _():
        carry[...] = jnp.zeros_like(carry)

    sel = sel_ref[...]
    selb = sel.astype(BF16)
    before = jnp.where(lax.broadcasted_iota(I32, (T, T), 0) < lax.broadcasted_iota(I32, (T, T), 1),
                       1.0, 0.0).astype(BF16)
    rank = jnp.dot(selb, before, preferred_element_type=F32)
    destf = ps_ref[...] + carry[...] + rank
    lower = jnp.where(lax.broadcasted_iota(I32, (E, E), 1) < lax.broadcasted_iota(I32, (E, E), 0),
                      1.0, 0.0).astype(BF16)
    slot = jnp.dot(lower, selb, preferred_element_type=F32)
    wf = wf_ref[...]
    for k in range(TOP_K):
        mk = jnp.where(slot == k, sel, 0.0)
        dest_ref[k:k + 1, :] = jnp.sum(mk * destf, axis=0, keepdims=True).astype(I32)
        wk_ref[k:k + 1, :] = jnp.sum(mk * wf, axis=0, keepdims=True)
    carry[...] += jnp.sum(sel, axis=1, keepdims=True)


def _rank(sel, wf, pad_start):
    e, n = sel.shape
    T = min(512, n)
    kern = functools.partial(_rank_kernel, T=T, E=e)
    return pl.pallas_call(
        kern,
        out_shape=(jax.ShapeDtypeStruct((TOP_K, n), I32), jax.ShapeDtypeStruct((TOP_K, n), F32)),
        grid=(n // T,),
        in_specs=[
            pl.BlockSpec((e, T), lambda i: (0, i)),
            pl.BlockSpec((e, T), lambda i: (0, i)),
            pl.BlockSpec((e, 1), lambda i: (0, 0)),
        ],
        out_specs=(pl.BlockSpec((TOP_K, T), lambda i: (0, i)),
                   pl.BlockSpec((TOP_K, T), lambda i: (0, i))),
        scratch_shapes=[pltpu.VMEM((e, 1), F32)],
        compiler_params=_cparams(("arbitrary",)),
        name="moe_rank",
    )(sel, wf, pad_start)


def _dispatch_kernel(zs_ref, zl_ref, dest_ref, h_ref, xs_ref, zbuf, sem, zsem, *, T, E):
    i = pl.program_id(0)

    def row_copy(t, k):
        return pltpu.make_async_copy(h_ref.at[pl.ds(t, 1), :],
                                     xs_ref.at[pl.ds(dest_ref[k, t], 1), :], sem)

    def for_each_pad_piece(fn):
        def per_expert(e, _):
            start = zs_ref[e]
            ln = zl_ref[e]
            head = jnp.minimum((8 - (start & 7)) & 7, ln)

            def per_row(r, _):
                fn(pltpu.make_async_copy(zbuf.at[pl.ds(0, 1), :],
                                         xs_ref.at[pl.ds(start + r, 1), :], zsem))
                return 0
            lax.fori_loop(0, head, per_row, 0)

            def per_group(q, _):
                r0 = pl.multiple_of(start + head + q * 8, 8)
                fn(pltpu.make_async_copy(zbuf, xs_ref.at[pl.ds(r0, 8), :], zsem))
                return 0
            lax.fori_loop(0, (ln - head) // 8, per_group, 0)
            return 0
        lax.fori_loop(0, E, per_expert, 0)

    @pl.when(i == 0)
    def _():
        zbuf[...] = jnp.zeros_like(zbuf)
        for_each_pad_piece(lambda cp: cp.start())

    def issue(t, _):
        for k in range(TOP_K):
            row_copy(t, k).start()
        return 0
    lax.fori_loop(0, T, issue, 0)

    for k in range(TOP_K):
        pltpu.make_async_copy(h_ref, xs_ref.at[pl.ds(0, T), :], sem).wait()

    @pl.when(i == 0)
    def _():
        for_each_pad_piece(lambda cp: cp.wait())


def _dispatch(h2, dest, zero_start, zero_len, n_rows):
    n, d2 = h2.shape
    e = zero_start.shape[0]
    T = min(256, n)
    kern = functools.partial(_dispatch_kernel, T=T, E=e)
    grid_spec = pltpu.PrefetchScalarGridSpec(
        num_scalar_prefetch=2,
        grid=(n // T,),
        in_specs=[
            pl.BlockSpec((TOP_K, T), lambda i, zs, zl: (0, i), memory_space=pltpu.SMEM),
            pl.BlockSpec((T, d2), lambda i, zs, zl: (i, 0)),
        ],
        out_specs=pl.BlockSpec(memory_space=pl.ANY),
        scratch_shapes=[
            pltpu.VMEM((8, d2), U32),
            pltpu.SemaphoreType.DMA(()),
            pltpu.SemaphoreType.DMA(()),
        ],
    )
    return pl.pallas_call(
        kern,
        out_shape=jax.ShapeDtypeStruct((n_rows, d2), U32),
        grid_spec=grid_spec,
        compiler_params=_cparams(("arbitrary",)),
        name="moe_dispatch",
    )(zero_start, zero_len, dest, h2)


def _cast_kernel(wg_ref, wu_ref, wd_ref, wgu_ref, wdo_ref, *, F):
    wgu_ref[:, :F] = wg_ref[...].astype(BF16)
    wgu_ref[:, F:] = wu_ref[...].astype(BF16)
    wdo_ref[...] = wd_ref[...].astype(BF16)


def _cast_experts(w_gate, w_up, w_down, layer):
    _, e, d, f = w_gate.shape
    nh = 2
    dh = d // nh
    kern = functools.partial(_cast_kernel, F=f)
    return pl.pallas_call(
        kern,
        out_shape=(jax.ShapeDtypeStruct((e, d, 2 * f), BF16), jax.ShapeDtypeStruct((e, f, d), BF16)),
        grid=(e, nh),
        in_specs=[
            pl.BlockSpec((None, None, dh, f), lambda i, h: (layer, i, h, 0)),
            pl.BlockSpec((None, None, dh, f), lambda i, h: (layer, i, h, 0)),
            pl.BlockSpec((None, None, f, dh), lambda i, h: (layer, i, 0, h)),
        ],
        out_specs=(pl.BlockSpec((None, dh, 2 * f), lambda i, h: (i, h, 0)),
                   pl.BlockSpec((None, f, dh), lambda i, h: (i, 0, h))),
        compiler_params=_cparams(("arbitrary", "arbitrary")),
        name="expert_cast",
    )(w_gate, w_up, w_down)


def _ffn_kernel(be_ref, na_ref, x_ref, wgu_ref, wd_ref, y_ref, *, F, D2):
    b = pl.program_id(0)

    @pl.when(b < na_ref[0])
    def _():
        rows = x_ref.shape[0] // FFN_CHAINS
        for s in range(FFN_CHAINS):
            rs = slice(s * rows, (s + 1) * rows)
            lo, hi = _unpack_pair(x_ref[rs, :])
            gu = (jnp.dot(lo.astype(BF16), wgu_ref[0:D2, :], preferred_element_type=F32)
                  + jnp.dot(hi.astype(BF16), wgu_ref[D2:2 * D2, :], preferred_element_type=F32))
            act = jax.nn.silu(gu[:, :F]) * gu[:, F:]
            y = jnp.dot(act.astype(BF16), wd_ref[...], preferred_element_type=F32)
            y_ref[rs, :] = _pack_pair(y[:, :D2], y[:, D2:])

    @pl.when(b >= na_ref[0])
    def _():
        y_ref[...] = jnp.zeros_like(y_ref)


def _grouped_ffn(xs, block_expert, n_active, wgu, wd):
    rows, d2 = xs.shape
    e, d, f2 = wgu.shape
    f = f2 // 2
    nb = rows // EXPERT_BLOCK
    kern = functools.partial(_ffn_kernel, F=f, D2=d2)

    def blk(b, be, na):
        return jnp.minimum(b, na[0] - 1)

    grid_spec = pltpu.PrefetchScalarGridSpec(
        num_scalar_prefetch=2,
        grid=(nb,),
        in_specs=[
            pl.BlockSpec((EXPERT_BLOCK, d2), lambda b, be, na: (blk(b, be, na), 0)),
            pl.BlockSpec((None, d, f2), lambda b, be, na: (be[blk(b, be, na)], 0, 0)),
            pl.BlockSpec((None, f, d), lambda b, be, na: (be[blk(b, be, na)], 0, 0)),
        ],
        out_specs=pl.BlockSpec((EXPERT_BLOCK, d2), lambda b, be, na: (b, 0)),
    )
    return pl.pallas_call(
        kern,
        out_shape=jax.ShapeDtypeStruct((rows, d2), U32),
        grid_spec=grid_spec,
        compiler_params=_cparams(("arbitrary",)),
        name="moe_ffn",
    )(block_expert, n_active, xs, wgu, wd)


def _combine_kernel(dcur_ref, dnext_ref, wk_ref, ys_ref, ysh_ref, x_ref, gate_ref, gain_ref, o_ref,
                    buf, sems, *, T, D, NT, CW, FINAL):
    i = pl.program_id(0)
    slot = lax.rem(i, 2)
    D2 = D // 2

    def issue(dref, sl, t):
        for k in range(TOP_K):
            pltpu.make_async_copy(ys_ref.at[pl.ds(dref[k, t], 1), :],
                                  buf.at[sl, k, pl.ds(t, 1), :], sems.at[sl]).start()

    def wait_slot(sl):
        for k in range(TOP_K):
            pltpu.make_async_copy(ys_ref.at[pl.ds(0, T), :], buf.at[sl, k], sems.at[sl]).wait()

    @pl.when(i == 0)
    def _():
        def body(t, _):
            issue(dcur_ref, 0, t)
            return 0
        lax.fori_loop(0, T, body, 0)

    wait_slot(slot)

    def row_group(rg, _):
        for tt in range(8):
            issue(dnext_ref, 1 - slot, rg * 8 + tt)
        r = pl.ds(pl.multiple_of(rg * 8, 8), 8)
        w8 = wk_ref[r, :]
        wk = [w8[:, k:k + 1] for k in range(TOP_K)]
        ss = jnp.zeros((8, 1), F32)
        for c in range(D2 // CW):
            c_lo = slice(c * CW, (c + 1) * CW)
            c_hi = slice(D2 + c * CW, D2 + (c + 1) * CW)
            lo, hi = _unpack_pair(ysh_ref[r, c_lo])
            for k in range(TOP_K):
                lo_k, hi_k = _unpack_pair(buf[slot, k, r, c_lo])
                lo = lo + wk[k] * lo_k
                hi = hi + wk[k] * hi_k
            xo_lo = x_ref[r, c_lo] + gate_ref[:, c_lo] * lo
            xo_hi = x_ref[r, c_hi] + gate_ref[:, c_hi] * hi
            if FINAL:
                ss = ss + (jnp.sum(xo_lo * xo_lo, axis=-1, keepdims=True)
                           + jnp.sum(xo_hi * xo_hi, axis=-1, keepdims=True))
            o_ref[r, c_lo] = xo_lo
            o_ref[r, c_hi] = xo_hi
        if FINAL:
            inv = _rms_inv(ss, D)
            for c in range(D // CW):
                cs = slice(c * CW, (c + 1) * CW)
                o_ref[r, cs] = (o_ref[r, cs] * inv) * gain_ref[:, cs]
        return 0
    lax.fori_loop(0, T // 8, row_group, 0)

    @pl.when(i == NT - 1)
    def _():
        wait_slot(1 - slot)


def _combine(dest, wk_t, ys, ysh, x, gate, gain, final):
    n, d = x.shape
    d2 = d // 2
    T = min(128, n)
    nt = n // T
    kern = functools.partial(_combine_kernel, T=T, D=d, NT=nt, CW=min(512, d2), FINAL=final)
    return pl.pallas_call(
        kern,
        out_shape=jax.ShapeDtypeStruct((n, d), F32),
        grid=(nt,),
        in_specs=[
            pl.BlockSpec((TOP_K, T), lambda i: (0, i), memory_space=pltpu.SMEM),
            pl.BlockSpec((TOP_K, T), lambda i: (0, jnp.minimum(i + 1, nt - 1)),
                         memory_space=pltpu.SMEM),
            pl.BlockSpec((T, TOP_K), lambda i: (i, 0)),
            pl.BlockSpec(memory_space=pl.ANY),
            pl.BlockSpec((T, d2), lambda i: (i, 0)),
            pl.BlockSpec((T, d), lambda i: (i, 0)),
            pl.BlockSpec((1, d), lambda i: (0, 0)),
            pl.BlockSpec((1, d), lambda i: (0, 0)),
        ],
        out_specs=pl.BlockSpec((T, d), lambda i: (i, 0)),
        scratch_shapes=[
            pltpu.VMEM((2, TOP_K, T, d2), U32),
            pltpu.SemaphoreType.DMA((2,)),
        ],
        compiler_params=_cparams(("arbitrary",)),
        name="moe_combine",
    )(dest, dest, wk_t, ys, ysh, x, gate, gain)


def _moe(h2, x, gate, gain, final, layer, router_w, router_bias, w_gate, w_up, w_down,
         ws_gate, ws_up, ws_down):
    n, d2 = h2.shape
    e = router_w.shape[1]
    blk = EXPERT_BLOCK

    rw_t = router_w.T.astype(BF16)
    sel, wf, cnt = _router(h2, rw_t[:, :d2], rw_t[:, d2:], router_bias.reshape(e, 1))

    counts = jnp.sum(cnt, axis=1).astype(I32)
    padded = (counts + blk - 1) // blk * blk
    pad_end = jnp.cumsum(padded)
    pad_start = pad_end - padded
    n_blocks = (n * TOP_K + e * (blk - 1) + blk - 1) // blk
    block_start = jnp.arange(n_blocks, dtype=I32) * blk
    block_expert = jnp.minimum(
        jnp.sum((pad_end[None, :] <= block_start[:, None]).astype(I32), axis=1), e - 1)
    n_active = (pad_end[-1:] // blk).astype(I32)

    dest, wk = _rank(sel, wf, pad_start.astype(F32).reshape(e, 1))
    zero_len = (padded - counts).at[e - 1].add(n_blocks * blk - pad_end[-1])
    xs = _dispatch(h2, dest, pad_start + counts, zero_len, n_blocks * blk)

    wgu, wd = _cast_experts(w_gate, w_up, w_down, layer)
    ys = _grouped_ffn(xs, block_expert, n_active, wgu, wd)

    wsgu, wsd = _cast_experts(ws_gate[:, None], ws_up[:, None], ws_down[:, None], layer)
    ysh = _grouped_ffn(h2, jnp.zeros((n // blk,), I32), jnp.full((1,), n // blk, I32), wsgu, wsd)

    return _combine(dest, wk.T, ys, ysh, x, gate, gain, final)


def kernel(x, c, mod_w, mod_b, lru_w_in, lru_b_in, lru_conv_w, lru_conv_b, lru_gate_a_w,
           lru_gate_a_b, lru_gate_x_w, lru_gate_x_b, lru_lambda, lru_w_out, lru_b_out, pool_w,
           pool_b, pool_scale, router_w, router_bias, expert_w_gate, expert_w_up, expert_w_down,
           shared_w_gate, shared_w_up, shared_w_down, final_gain):
    bsz, seq, d = x.shape
    depth = mod_w.shape[0]
    assert bsz == 1, "one sequence per call"
    xf = x.reshape(seq, d)
    row = lambda v: v.reshape(1, -1)

    mod = _adaln_all(c, mod_w, mod_b)

    def adaln(i, s):
        m = mod[2 * i + s]
        return m[:, :d], m[:, d:2 * d], m[:, 2 * d:]

    gain = row(final_gain)
    for i in range(depth):
        j = i // 2
        shift, scale, gate = adaln(i, 0)
        shift2, scale2, gate2 = adaln(i, 1)
        if i % 2 == 0:
            h = _norm_mod(xf, scale, shift)
            g, xc = _inproj(h, lru_w_in[j].astype(BF16), row(lru_b_in[j]),
                            lru_conv_w[j], row(lru_conv_b[j]))
            z = _lru_scan(xc, g, lru_gate_a_w[j].astype(BF16), row(lru_gate_a_b[j]),
                          lru_gate_x_w[j].astype(BF16), row(lru_gate_x_b[j]), row(lru_lambda[j]))
            xf, h2 = _outproj(z, lru_w_out[j].astype(BF16), row(lru_b_out[j]), xf, gate,
                              scale2, shift2)
        else:
            xf, h2 = _pool_mixer(xf, scale, shift, pool_w[j].astype(BF16), row(pool_b[j]),
                                 row(pool_scale[j]), gate, scale2, shift2)
        xf = _moe(h2, xf, gate2, gain, i == depth - 1, i, router_w[i], router_bias[i],
                  expert_w_gate, expert_w_up, expert_w_down,
                  shared_w_gate, shared_w_up, shared_w_down)
    return xf.reshape(bsz, seq, d)
```

```python
import functools

import jax
import jax.numpy as jnp
from jax import lax
from jax.experimental import pallas as pl
from jax.experimental.pallas import tpu as pltpu

F32 = jnp.float32
BF16 = jnp.bfloat16
U32 = jnp.uint32
I32 = jnp.int32

EPS = 1e-6
LRU_C = 8.0
N_GROUPS = 8
TOPK_GROUPS = 4
TOP_K = 8
ROUTED_SCALE = 2.5
POOL_WINDOWS = (2, 4, 8, 16)
EXPERT_BLOCK = 512
FFN_CHAINS = 4
POOL_HIST = 24
VMEM_LIMIT = 56 * 1024 * 1024


def _cparams(sem, vmem=VMEM_LIMIT):
    return pltpu.CompilerParams(dimension_semantics=sem, vmem_limit_bytes=vmem)


def _pack_pair(lo, hi):
    lo_b = lax.bitcast_convert_type(lo.astype(BF16).astype(F32), U32)
    hi_b = lax.bitcast_convert_type(hi.astype(BF16).astype(F32), U32)
    return hi_b | (lo_b >> 16)


def _unpack_pair(w):
    lo = lax.bitcast_convert_type(w << 16, F32)
    hi = lax.bitcast_convert_type(w & jnp.uint32(0xFFFF0000), F32)
    return lo, hi


def _rms_inv(ss, d):
    return lax.rsqrt(ss / d + EPS)


CAST_SPLITS = (4, 2, 1)


def _cast_body(wg_ref, wu_ref, wd_ref, wgu_ref, wdo_ref):
    f = wg_ref.shape[-1]
    wgu_ref[:, :f] = wg_ref[...].astype(BF16)
    wgu_ref[:, f:] = wu_ref[...].astype(BF16)
    wdo_ref[...] = wd_ref[...].astype(BF16)


def _cast_plan(w_gate, layer, n_steps, step_fn):
    _, e, d, f = w_gate.shape
    q = next((s for s in CAST_SPLITS
              if e * s <= n_steps and f % s == 0 and d % s == 0 and (f // s) % 16 == 0), None)
    if q is None:
        return None
    dq, fq = d // q, f // q

    def piece(*a):
        u = jnp.minimum(step_fn(*a), e * q - 1)
        return u // q, u % q

    def w_in(*a):
        ex, pc = piece(*a)
        return layer, ex, pc, 0

    def w_out(*a):
        ex, pc = piece(*a)
        return ex, pc, 0

    in_specs = [pl.BlockSpec((None, None, dq, f), w_in), pl.BlockSpec((None, None, dq, f), w_in),
                pl.BlockSpec((None, None, fq, d), w_in)]
    out_specs = [pl.BlockSpec((None, dq, 2 * f), w_out), pl.BlockSpec((None, fq, d), w_out)]
    out_shapes = [jax.ShapeDtypeStruct((e, d, 2 * f), BF16), jax.ShapeDtypeStruct((e, f, d), BF16)]
    return in_specs, out_specs, out_shapes


def _cast_experts(w_gate, w_up, w_down, layer):
    e = w_gate.shape[1]
    n_steps = e * CAST_SPLITS[0]
    in_specs, out_specs, out_shapes = _cast_plan(w_gate, layer, n_steps, lambda i: i)
    return pl.pallas_call(
        _cast_body,
        out_shape=tuple(out_shapes),
        grid=(n_steps,),
        in_specs=in_specs,
        out_specs=tuple(out_specs),
        compiler_params=_cparams(("arbitrary",)),
        name="expert_cast",
    )(w_gate, w_up, w_down)


def _mod_kernel(c_ref, w_ref, b_ref, o_ref):
    c = c_ref[...]
    s = c * jax.nn.sigmoid(c)
    s8 = jnp.broadcast_to(s, (8, s.shape[1])).astype(BF16)
    acc = jnp.dot(s8, w_ref[...].astype(BF16), preferred_element_type=F32)
    o_ref[...] = acc[0:1, :] + b_ref[...]


def _adaln_all(c, mod_w, mod_b):
    depth, two, d, d3 = mod_w.shape
    na = depth * two
    tn = 512
    w = mod_w.reshape(na, d, d3)
    b = mod_b.reshape(na, 1, d3)
    return pl.pallas_call(
        _mod_kernel,
        out_shape=jax.ShapeDtypeStruct((na, 1, d3), F32),
        grid=(na, d3 // tn),
        in_specs=[
            pl.BlockSpec((1, d), lambda a, j: (0, 0)),
            pl.BlockSpec((None, d, tn), lambda a, j: (a, 0, j)),
            pl.BlockSpec((None, 1, tn), lambda a, j: (a, 0, j)),
        ],
        out_specs=pl.BlockSpec((None, 1, tn), lambda a, j: (a, 0, j)),
        compiler_params=_cparams(("arbitrary", "arbitrary")),
        name="adaln_mod",
    )(c, w, b)


def _norm_mod_kernel(x_ref, sc_ref, sh_ref, o_ref):
    x = x_ref[...]
    inv = _rms_inv(jnp.sum(x * x, axis=-1, keepdims=True), x.shape[-1])
    o_ref[...] = ((x * inv) * (1.0 + sc_ref[...]) + sh_ref[...]).astype(o_ref.dtype)


def _norm_mod(x, scale, shift):
    n, d = x.shape
    tm = min(512, n)
    return pl.pallas_call(
        _norm_mod_kernel,
        out_shape=jax.ShapeDtypeStruct((n, d), BF16),
        grid=(n // tm,),
        in_specs=[
            pl.BlockSpec((tm, d), lambda i: (i, 0)),
            pl.BlockSpec((1, d), lambda i: (0, 0)),
            pl.BlockSpec((1, d), lambda i: (0, 0)),
        ],
        out_specs=pl.BlockSpec((tm, d), lambda i: (i, 0)),
        compiler_params=_cparams(("arbitrary",)),
        name="norm_mod",
    )(x, scale, shift)


def _inproj_kernel(h_ref, wg_ref, wu_ref, bg_ref, bu_ref, g_ref, u_ref):
    h = h_ref[...]
    gate = jnp.dot(h, wg_ref[...], preferred_element_type=F32) + bg_ref[...]
    g_ref[...] = jax.nn.gelu(gate, approximate=True).astype(g_ref.dtype)
    u_ref[...] = jnp.dot(h, wu_ref[...], preferred_element_type=F32) + bu_ref[...]


def _inproj(h, w_in, b_in):
    n, d = h.shape
    dl = w_in.shape[1] // 2
    tm = min(1024, n)
    tn = min(512, dl)
    nj = dl // tn
    return pl.pallas_call(
        _inproj_kernel,
        out_shape=(jax.ShapeDtypeStruct((n, dl), BF16), jax.ShapeDtypeStruct((n, dl), F32)),
        grid=(n // tm, nj),
        in_specs=[
            pl.BlockSpec((tm, d), lambda i, j: (i, 0)),
            pl.BlockSpec((d, tn), lambda i, j: (0, j)),
            pl.BlockSpec((d, tn), lambda i, j: (0, j + nj)),
            pl.BlockSpec((1, tn), lambda i, j: (0, j)),
            pl.BlockSpec((1, tn), lambda i, j: (0, j + nj)),
        ],
        out_specs=(pl.BlockSpec((tm, tn), lambda i, j: (i, j)),
                   pl.BlockSpec((tm, tn), lambda i, j: (i, j))),
        compiler_params=_cparams(("arbitrary", "arbitrary")),
        name="lru_inproj",
    )(h, w_in, w_in, b_in, b_in)


def _lru_kernel(u_ref, g_ref, cw_ref, cb_ref, gaw_ref, gab_ref, gxw_ref, gxb_ref, lam_ref, *rest,
                T, CB, HD):
    uext, hcar, a_s, b_s = rest[-4:]
    z_ref = rest[-5] if len(rest) == 5 else rest[3]
    if len(rest) > 5:
        _cast_body(*rest[:3], *rest[4:6])
    t = pl.program_id(1)

    @pl.when(t == 0)
    def _():
        uext[0:8, :] = jnp.zeros((8, CB), F32)
        hcar[...] = jnp.zeros((8, CB), F32)

    uext[8:T + 8, :] = u_ref[...]
    cw = cw_ref[...]
    xc = (cb_ref[...] + cw[0:1, :] * uext[5:T + 5, :] + cw[1:2, :] * uext[6:T + 6, :]
          + cw[2:3, :] * uext[7:T + 7, :] + cw[3:4, :] * uext[8:T + 8, :])
    uext[0:8, :] = uext[T:T + 8, :]

    xcb = xc.astype(BF16)
    rs, xs = [], []
    for h in range(CB // HD):
        xh = xcb[:, h * HD:(h + 1) * HD]
        rs.append(jnp.dot(xh, gaw_ref[h], preferred_element_type=F32))
        xs.append(jnp.dot(xh, gxw_ref[h], preferred_element_type=F32))
    r = jax.nn.sigmoid(jnp.concatenate(rs, axis=1) + gab_ref[...])
    i = jax.nn.sigmoid(jnp.concatenate(xs, axis=1) + gxb_ref[...])
    lam = lam_ref[...]
    log_sig = jnp.minimum(lam, 0.0) - jnp.log(1.0 + jnp.exp(-jnp.abs(lam)))
    a = jnp.exp((LRU_C * r) * log_sig)
    b = jnp.sqrt(1.0 - a * a) * (i * xc)

    G = T // 8
    A = a.reshape(G, 8, CB)
    B = b.reshape(G, 8, CB)
    row = lax.broadcasted_iota(I32, (G, 8, CB), 1)
    for d in (1, 2, 4):
        keep = row >= d
        B = jnp.where(keep, A * pltpu.roll(B, d, axis=1) + B, B)
        A = jnp.where(keep, A * pltpu.roll(A, d, axis=1), A)
    a_s[...] = A.reshape(T, CB)
    b_s[...] = B.reshape(T, CB)

    def carry_step(gi, hprev):
        r0 = pl.multiple_of(gi * 8, 8)
        hrows = b_s[pl.ds(r0, 8), :] + a_s[pl.ds(r0, 8), :] * hprev
        b_s[pl.ds(r0, 8), :] = hrows
        return jnp.broadcast_to(hrows[7:8, :], (8, CB))

    hcar[...] = lax.fori_loop(0, G, carry_step, hcar[...])
    z_ref[...] = (b_s[...] * g_ref[...].astype(F32)).astype(z_ref.dtype)


def _lru_scan(u, g, conv_w, conv_b, ga_w, ga_b, gx_w, gx_b, lam, cast_src=None):
    n, dl = u.shape
    nh, hd, _ = ga_w.shape
    T = min(256, n)
    CB = min(1024, dl)
    hpb = CB // hd
    nt = n // T
    vec = lambda: pl.BlockSpec((1, CB), lambda c, t: (0, c))
    kern = functools.partial(_lru_kernel, T=T, CB=CB, HD=hd)
    in_specs = [
        pl.BlockSpec((T, CB), lambda c, t: (t, c)),
        pl.BlockSpec((T, CB), lambda c, t: (t, c)),
        pl.BlockSpec((conv_w.shape[0], CB), lambda c, t: (0, c)),
        vec(),
        pl.BlockSpec((hpb, hd, hd), lambda c, t: (c, 0, 0)),
        vec(),
        pl.BlockSpec((hpb, hd, hd), lambda c, t: (c, 0, 0)),
        vec(),
        vec(),
    ]
    out_specs = [pl.BlockSpec((T, CB), lambda c, t: (t, c))]
    out_shapes = [jax.ShapeDtypeStruct((n, dl), BF16)]
    args = [u, g, conv_w, conv_b, ga_w, ga_b, gx_w, gx_b, lam]
    plan = None
    if cast_src is not None:
        plan = _cast_plan(cast_src[0], cast_src[3], (dl // CB) * nt, lambda c, t: c * nt + t)
    if plan is not None:
        in_specs += plan[0]
        out_specs += plan[1]
        out_shapes += plan[2]
        args += list(cast_src[:3])
    outs = pl.pallas_call(
        kern,
        out_shape=tuple(out_shapes),
        grid=(dl // CB, nt),
        in_specs=in_specs,
        out_specs=tuple(out_specs),
        scratch_shapes=[
            pltpu.VMEM((T + 8, CB), F32),
            pltpu.VMEM((8, CB), F32),
            pltpu.VMEM((T, CB), F32),
            pltpu.VMEM((T, CB), F32),
        ],
        compiler_params=_cparams(("arbitrary", "arbitrary")),
        name="lru_scan",
    )(*args)
    return outs[0], (tuple(outs[1:]) if plan is not None else None)


def _norm_pack_store(acc, sc_ref, sh_ref, h_ref, nj, tn, d):
    ss = jnp.sum(acc[0] * acc[0], axis=-1, keepdims=True)
    for jj in range(1, nj):
        ss = ss + jnp.sum(acc[jj] * acc[jj], axis=-1, keepdims=True)
    inv = _rms_inv(ss, d)
    half = nj // 2
    for jj in range(half):
        lo_c = slice(jj * tn, (jj + 1) * tn)
        hi_c = slice((jj + half) * tn, (jj + half + 1) * tn)
        lo = (acc[jj] * inv) * (1.0 + sc_ref[:, lo_c]) + sh_ref[:, lo_c]
        hi = (acc[jj + half] * inv) * (1.0 + sc_ref[:, hi_c]) + sh_ref[:, hi_c]
        h_ref[:, lo_c] = _pack_pair(lo, hi)


def _outproj_kernel(z_ref, w_ref, b_ref, x_ref, gate_ref, sc_ref, sh_ref, x1_ref, h_ref, acc,
                    *, NJ, TN, D):
    j = pl.program_id(1)
    y = jnp.dot(z_ref[...], w_ref[...], preferred_element_type=F32) + b_ref[...]
    x1 = x_ref[...] + gate_ref[...] * y
    x1_ref[...] = x1
    acc[j] = x1

    @pl.when(j == NJ - 1)
    def _():
        _norm_pack_store(acc, sc_ref, sh_ref, h_ref, NJ, TN, D)


def _outproj(z, w_out, b_out, x, gate, scale, shift):
    n, dl = z.shape
    d = w_out.shape[1]
    tm = min(512, n)
    tn = min(1024, d // 2)
    nj = d // tn
    kern = functools.partial(_outproj_kernel, NJ=nj, TN=tn, D=d)
    return pl.pallas_call(
        kern,
        out_shape=(jax.ShapeDtypeStruct((n, d), F32), jax.ShapeDtypeStruct((n, d // 2), U32)),
        grid=(n // tm, nj),
        in_specs=[
            pl.BlockSpec((tm, dl), lambda i, j: (i, 0)),
            pl.BlockSpec((dl, tn), lambda i, j: (0, j)),
            pl.BlockSpec((1, tn), lambda i, j: (0, j)),
            pl.BlockSpec((tm, tn), lambda i, j: (i, j)),
            pl.BlockSpec((1, tn), lambda i, j: (0, j)),
            pl.BlockSpec((1, d), lambda i, j: (0, 0)),
            pl.BlockSpec((1, d), lambda i, j: (0, 0)),
        ],
        out_specs=(pl.BlockSpec((tm, tn), lambda i, j: (i, j)),
                   pl.BlockSpec((tm, d // 2), lambda i, j: (i, 0))),
        scratch_shapes=[pltpu.VMEM((nj, tm, tn), F32)],
        compiler_params=_cparams(("arbitrary", "arbitrary")),
        name="lru_outproj",
    )(z, w_out, b_out, x, gate, scale, shift)


def _pool_kernel(x_ref, sc_ref, sh_ref, pw_ref, pb_ref, ps_ref, gate_ref, sc2_ref, sh2_ref,
                 x3_ref, h_ref, ext, s_a, s_b, ybuf, *, T, D, GD):
    t = pl.program_id(0)
    H = POOL_HIST

    @pl.when(t == 0)
    def _():
        ext[0:H, :] = jnp.zeros((H, D), F32)
        s_a[0:8, :] = jnp.zeros((8, GD), F32)
        s_b[0:8, :] = jnp.zeros((8, GD), F32)

    x = x_ref[...]
    inv = _rms_inv(jnp.sum(x * x, axis=-1, keepdims=True), D)
    ext[H:T + H, :] = (x * inv) * (1.0 + sc_ref[...]) + sh_ref[...]

    pos1 = lax.broadcasted_iota(I32, (T, 1), 0) + (t * T + 1)
    for g, w in enumerate(POOL_WINDOWS):
        c = slice(g * GD, (g + 1) * GD)
        if w == 2:
            win = ext[H:T + H, c] + ext[H - 1:T + H - 1, c]
        else:
            s_a[8:T + H, :] = ext[8:T + H, c] + ext[7:T + H - 1, c]
            if w == 4:
                win = s_a[H:T + H, :] + s_a[H - 2:T + H - 2, :]
            else:
                s_b[8:T + H, :] = s_a[8:T + H, :] + s_a[6:T + H - 2, :]
                if w == 8:
                    win = s_b[H:T + H, :] + s_b[H - 4:T + H - 4, :]
                else:
                    s_a[8:T + H, :] = s_b[8:T + H, :] + s_b[4:T + H - 4, :]
                    win = s_a[H:T + H, :] + s_a[H - 8:T + H - 8, :]
        cnt = jnp.minimum(pos1, w).astype(F32)
        pooled = win / cnt - ext[H:T + H, c]
        yg = jnp.dot(pooled.astype(BF16), pw_ref[g], preferred_element_type=F32) + pb_ref[:, c]
        ybuf[:, c] = yg * ps_ref[:, c]
    ext[0:H, :] = ext[T:T + H, :]

    x3 = x + gate_ref[...] * ybuf[...]
    x3_ref[...] = x3
    inv3 = _rms_inv(jnp.sum(x3 * x3, axis=-1, keepdims=True), D)
    h4 = (x3 * inv3) * (1.0 + sc2_ref[...]) + sh2_ref[...]
    h_ref[...] = _pack_pair(h4[:, :D // 2], h4[:, D // 2:])


def _pool_mixer(x, scale, shift, pool_w, pool_b, pool_scale, gate, scale2, shift2):
    n, d = x.shape
    ng, gd, _ = pool_w.shape
    T = min(256, n)
    vec = lambda: pl.BlockSpec((1, d), lambda t: (0, 0))
    kern = functools.partial(_pool_kernel, T=T, D=d, GD=gd)
    return pl.pallas_call(
        kern,
        out_shape=(jax.ShapeDtypeStruct((n, d), F32), jax.ShapeDtypeStruct((n, d // 2), U32)),
        grid=(n // T,),
        in_specs=[
            pl.BlockSpec((T, d), lambda t: (t, 0)),
            vec(), vec(),
            pl.BlockSpec((ng, gd, gd), lambda t: (0, 0, 0)),
            vec(), vec(), vec(), vec(), vec(),
        ],
        out_specs=(pl.BlockSpec((T, d), lambda t: (t, 0)),
                   pl.BlockSpec((T, d // 2), lambda t: (t, 0))),
        scratch_shapes=[
            pltpu.VMEM((T + POOL_HIST, d), F32),
            pltpu.VMEM((T + POOL_HIST, gd), F32),
            pltpu.VMEM((T + POOL_HIST, gd), F32),
            pltpu.VMEM((T, d), F32),
        ],
        compiler_params=_cparams(("arbitrary",)),
        name="pool_mixer",
    )(x, scale, shift, pool_w, pool_b, pool_scale, gate, scale2, shift2)


def _router_kernel(h_ref, wlo_ref, whi_ref, rb_ref, sel_ref, wf_ref, cnt_ref, *, T, E):
    i = pl.program_id(0)
    NG = N_GROUPS
    GS = E // NG
    lo, hi = _unpack_pair(h_ref[...])
    dn = (((1,), (1,)), ((), ()))
    logits = (lax.dot_general(wlo_ref[...], lo.astype(BF16), dn, preferred_element_type=F32)
              + lax.dot_general(whi_ref[...], hi.astype(BF16), dn, preferred_element_type=F32))
    scores = jax.nn.sigmoid(logits).reshape(NG, GS, T)
    biased = scores + rb_ref[...].reshape(NG, GS, 1)

    sub = lax.broadcasted_iota(I32, (NG, GS, T), 1)
    m1 = jnp.max(biased, axis=1, keepdims=True)
    first = jnp.min(jnp.where(biased == m1, sub, GS), axis=1, keepdims=True)
    m2 = jnp.max(jnp.where(sub == first, -jnp.inf, biased), axis=1, keepdims=True)
    gscore = m1 + m2

    gidx = lax.broadcasted_iota(I32, (NG, 1, T), 0)
    beaten = jnp.zeros((NG, 1, T), F32)
    for j in range(NG):
        sj = gscore[j:j + 1]
        beaten = beaten + jnp.where(sj > gscore, 1.0,
                                    jnp.where(sj == gscore, jnp.where(gidx > j, 1.0, 0.0), 0.0))
    masked = jnp.where(beaten < TOPK_GROUPS, biased, -jnp.inf)

    eidx = lax.broadcasted_iota(I32, (NG, GS, T), 0) * GS + sub
    beaten = jnp.zeros((NG, GS, T), F32)
    for g in range(NG):
        mg = masked[g]
        for s in range(GS):
            v = mg[s:s + 1, :][None]
            later = jnp.where(eidx > g * GS + s, 1.0, 0.0)
            beaten = beaten + jnp.where(v > masked, 1.0, jnp.where(v == masked, later, 0.0))
    sel = jnp.where(beaten < TOP_K, 1.0, 0.0)

    picked = sel * scores
    tot = jnp.sum(jnp.sum(picked, axis=1, keepdims=True), axis=0, keepdims=True)
    wf = picked / tot * ROUTED_SCALE

    sel2 = sel.reshape(E, T)
    sel_ref[...] = sel2
    wf_ref[...] = wf.reshape(E, T)

    @pl.when(i == 0)
    def _():
        cnt_ref[...] = jnp.zeros_like(cnt_ref)
    part = sel2[:, 0:128]
    for c in range(1, T // 128):
        part = part + sel2[:, c * 128:(c + 1) * 128]
    cnt_ref[...] += part


def _router(h2, wlo, whi, rbias):
    n, d2 = h2.shape
    e = wlo.shape[0]
    T = min(512, n)
    kern = functools.partial(_router_kernel, T=T, E=e)
    return pl.pallas_call(
        kern,
        out_shape=(jax.ShapeDtypeStruct((e, n), F32), jax.ShapeDtypeStruct((e, n), F32),
                   jax.ShapeDtypeStruct((e, 128), F32)),
        grid=(n // T,),
        in_specs=[
            pl.BlockSpec((T, d2), lambda i: (i, 0)),
            pl.BlockSpec((e, d2), lambda i: (0, 0)),
            pl.BlockSpec((e, d2), lambda i: (0, 0)),
            pl.BlockSpec((e, 1), lambda i: (0, 0)),
        ],
        out_specs=(pl.BlockSpec((e, T), lambda i: (0, i)),
                   pl.BlockSpec((e, T), lambda i: (0, i)),
                   pl.BlockSpec((e, 128), lambda i: (0, 0))),
        compiler_params=_cparams(("arbitrary",)),
        name="moe_router",
    )(h2, wlo, whi, rbias)


def _rank_kernel(sel_ref, wf_ref, ps_ref, dest_ref, wk_ref, carry, *, T, E):
    i = pl.program_id(0)

    @pl.when(i == 0)
    def _():
        carry[...] = jnp.zeros_like(carry)

    sel = sel_ref[...]
    selb = sel.astype(BF16)
    before = jnp.where(lax.broadcasted_iota(I32, (T, T), 0) < lax.broadcasted_iota(I32, (T, T), 1),
                       1.0, 0.0).astype(BF16)
    rank = jnp.dot(selb, before, preferred_element_type=F32)
    destf = ps_ref[...] + carry[...] + rank
    lower = jnp.where(lax.broadcasted_iota(I32, (E, E), 1) < lax.broadcasted_iota(I32, (E, E), 0),
                      1.0, 0.0).astype(BF16)
    slot = jnp.dot(lower, selb, preferred_element_type=F32)
    wf = wf_ref[...]
    for k in range(TOP_K):
        mk = jnp.where(slot == k, sel, 0.0)
        dest_ref[k:k + 1, :] = jnp.sum(mk * destf, axis=0, keepdims=True).astype(I32)
        wk_ref[k:k + 1, :] = jnp.sum(mk * wf, axis=0, keepdims=True)
    carry[...] += jnp.sum(sel, axis=1, keepdims=True)


def _rank(sel, wf, pad_start):
    e, n = sel.shape
    T = min(512, n)
    kern = functools.partial(_rank_kernel, T=T, E=e)
    return pl.pallas_call(
        kern,
        out_shape=(jax.ShapeDtypeStruct((TOP_K, n), I32), jax.ShapeDtypeStruct((TOP_K, n), F32)),
        grid=(n // T,),
        in_specs=[
            pl.BlockSpec((e, T), lambda i: (0, i)),
            pl.BlockSpec((e, T), lambda i: (0, i)),
            pl.BlockSpec((e, 1), lambda i: (0, 0)),
        ],
        out_specs=(pl.BlockSpec((TOP_K, T), lambda i: (0, i)),
                   pl.BlockSpec((TOP_K, T), lambda i: (0, i))),
        scratch_shapes=[pltpu.VMEM((e, 1), F32)],
        compiler_params=_cparams(("arbitrary",)),
        name="moe_rank",
    )(sel, wf, pad_start)


def _dispatch_kernel(zs_ref, zl_ref, dest_ref, h_ref, xs_ref, zbuf, sem, zsem, *, T, E):
    i = pl.program_id(0)

    def row_copy(t, k):
        return pltpu.make_async_copy(h_ref.at[pl.ds(t, 1), :],
                                     xs_ref.at[pl.ds(dest_ref[k, t], 1), :], sem)

    def for_each_pad_piece(fn):
        def per_expert(e, _):
            start = zs_ref[e]
            ln = zl_ref[e]
            head = jnp.minimum((8 - (start & 7)) & 7, ln)

            def per_row(r, _):
                fn(pltpu.make_async_copy(zbuf.at[pl.ds(0, 1), :],
                                         xs_ref.at[pl.ds(start + r, 1), :], zsem))
                return 0
            lax.fori_loop(0, head, per_row, 0)

            def per_group(q, _):
                r0 = pl.multiple_of(start + head + q * 8, 8)
                fn(pltpu.make_async_copy(zbuf, xs_ref.at[pl.ds(r0, 8), :], zsem))
                return 0
            lax.fori_loop(0, (ln - head) // 8, per_group, 0)
            return 0
        lax.fori_loop(0, E, per_expert, 0)

    @pl.when(i == 0)
    def _():
        zbuf[...] = jnp.zeros_like(zbuf)
        for_each_pad_piece(lambda cp: cp.start())

    def issue(t, _):
        for k in range(TOP_K):
            row_copy(t, k).start()
        return 0
    lax.fori_loop(0, T, issue, 0)

    for k in range(TOP_K):
        pltpu.make_async_copy(h_ref, xs_ref.at[pl.ds(0, T), :], sem).wait()

    @pl.when(i == 0)
    def _():
        for_each_pad_piece(lambda cp: cp.wait())


def _dispatch(h2, dest, zero_start, zero_len, n_rows):
    n, d2 = h2.shape
    e = zero_start.shape[0]
    T = min(256, n)
    kern = functools.partial(_dispatch_kernel, T=T, E=e)
    grid_spec = pltpu.PrefetchScalarGridSpec(
        num_scalar_prefetch=2,
        grid=(n // T,),
        in_specs=[
            pl.BlockSpec((TOP_K, T), lambda i, zs, zl: (0, i), memory_space=pltpu.SMEM),
            pl.BlockSpec((T, d2), lambda i, zs, zl: (i, 0)),
        ],
        out_specs=pl.BlockSpec(memory_space=pl.ANY),
        scratch_shapes=[
            pltpu.VMEM((8, d2), U32),
            pltpu.SemaphoreType.DMA(()),
            pltpu.SemaphoreType.DMA(()),
        ],
    )
    return pl.pallas_call(
        kern,
        out_shape=jax.ShapeDtypeStruct((n_rows, d2), U32),
        grid_spec=grid_spec,
        compiler_params=_cparams(("arbitrary",)),
        name="moe_dispatch",
    )(zero_start, zero_len, dest, h2)


def _ffn_kernel(be_ref, na_ref, x_ref, wgu_ref, wd_ref, *rest, F, D2):
    y_ref = rest[-3] if len(rest) > 1 else rest[0]
    if len(rest) > 1:
        _cast_body(*rest[:3], *rest[-2:])
    b = pl.program_id(0)

    @pl.when(b < na_ref[0])
    def _():
        rows = x_ref.shape[0] // FFN_CHAINS
        for s in range(FFN_CHAINS):
            rs = slice(s * rows, (s + 1) * rows)
            lo, hi = _unpack_pair(x_ref[rs, :])
            gu = (jnp.dot(lo.astype(BF16), wgu_ref[0:D2, :], preferred_element_type=F32)
                  + jnp.dot(hi.astype(BF16), wgu_ref[D2:2 * D2, :], preferred_element_type=F32))
            act = jax.nn.silu(gu[:, :F]) * gu[:, F:]
            y = jnp.dot(act.astype(BF16), wd_ref[...], preferred_element_type=F32)
            y_ref[rs, :] = _pack_pair(y[:, :D2], y[:, D2:])

    @pl.when(b >= na_ref[0])
    def _():
        y_ref[...] = jnp.zeros_like(y_ref)


def _grouped_ffn(xs, block_expert, n_active, wgu, wd, cast_src=None):
    rows, d2 = xs.shape
    e, d, f2 = wgu.shape
    f = f2 // 2
    nb = rows // EXPERT_BLOCK
    kern = functools.partial(_ffn_kernel, F=f, D2=d2)

    def blk(b, be, na):
        return jnp.minimum(b, na[0] - 1)

    in_specs = [
        pl.BlockSpec((EXPERT_BLOCK, d2), lambda b, be, na: (blk(b, be, na), 0)),
        pl.BlockSpec((None, d, f2), lambda b, be, na: (be[blk(b, be, na)], 0, 0)),
        pl.BlockSpec((None, f, d), lambda b, be, na: (be[blk(b, be, na)], 0, 0)),
    ]
    out_specs = [pl.BlockSpec((EXPERT_BLOCK, d2), lambda b, be, na: (b, 0))]
    out_shapes = [jax.ShapeDtypeStruct((rows, d2), U32)]
    args = [block_expert, n_active, xs, wgu, wd]
    plan = None
    if cast_src is not None:
        plan = _cast_plan(cast_src[0], cast_src[3], nb, lambda b, be, na: b)
    if plan is not None:
        in_specs += plan[0]
        out_specs += plan[1]
        out_shapes += plan[2]
        args += list(cast_src[:3])
    grid_spec = pltpu.PrefetchScalarGridSpec(
        num_scalar_prefetch=2, grid=(nb,), in_specs=in_specs, out_specs=tuple(out_specs))
    outs = pl.pallas_call(
        kern,
        out_shape=tuple(out_shapes),
        grid_spec=grid_spec,
        compiler_params=_cparams(("arbitrary",)),
        name="moe_ffn",
    )(*args)
    return outs[0], (tuple(outs[1:]) if plan is not None else None)


def _combine_kernel(dcur_ref, dnext_ref, wk_ref, ys_ref, ysh_ref, x_ref, gate_ref, gain_ref, o_ref,
                    buf, sems, *, T, D, NT, CW, FINAL):
    i = pl.program_id(0)
    slot = lax.rem(i, 2)
    D2 = D // 2

    def issue(dref, sl, t):
        for k in range(TOP_K):
            pltpu.make_async_copy(ys_ref.at[pl.ds(dref[k, t], 1), :],
                                  buf.at[sl, k, pl.ds(t, 1), :], sems.at[sl]).start()

    def wait_slot(sl):
        for k in range(TOP_K):
            pltpu.make_async_copy(ys_ref.at[pl.ds(0, T), :], buf.at[sl, k], sems.at[sl]).wait()

    @pl.when(i == 0)
    def _():
        def body(t, _):
            issue(dcur_ref, 0, t)
            return 0
        lax.fori_loop(0, T, body, 0)

    wait_slot(slot)

    def row_group(rg, _):
        for tt in range(8):
            issue(dnext_ref, 1 - slot, rg * 8 + tt)
        r = pl.ds(pl.multiple_of(rg * 8, 8), 8)
        w8 = wk_ref[r, :]
        wk = [w8[:, k:k + 1] for k in range(TOP_K)]
        ss = jnp.zeros((8, 1), F32)
        for c in range(D2 // CW):
            c_lo = slice(c * CW, (c + 1) * CW)
            c_hi = slice(D2 + c * CW, D2 + (c + 1) * CW)
            lo, hi = _unpack_pair(ysh_ref[r, c_lo])
            for k in range(TOP_K):
                lo_k, hi_k = _unpack_pair(buf[slot, k, r, c_lo])
                lo = lo + wk[k] * lo_k
                hi = hi + wk[k] * hi_k
            xo_lo = x_ref[r, c_lo] + gate_ref[:, c_lo] * lo
            xo_hi = x_ref[r, c_hi] + gate_ref[:, c_hi] * hi
            if FINAL:
                ss = ss + (jnp.sum(xo_lo * xo_lo, axis=-1, keepdims=True)
                           + jnp.sum(xo_hi * xo_hi, axis=-1, keepdims=True))
            o_ref[r, c_lo] = xo_lo
            o_ref[r, c_hi] = xo_hi
        if FINAL:
            inv = _rms_inv(ss, D)
            for c in range(D // CW):
                cs = slice(c * CW, (c + 1) * CW)
                o_ref[r, cs] = (o_ref[r, cs] * inv) * gain_ref[:, cs]
        return 0
    lax.fori_loop(0, T // 8, row_group, 0, unroll=4)

    @pl.when(i == NT - 1)
    def _():
        wait_slot(1 - slot)


def _combine(dest, wk_t, ys, ysh, x, gate, gain, final):
    n, d = x.shape
    d2 = d // 2
    T = min(128, n)
    nt = n // T
    kern = functools.partial(_combine_kernel, T=T, D=d, NT=nt, CW=min(512, d2), FINAL=final)
    return pl.pallas_call(
        kern,
        out_shape=jax.ShapeDtypeStruct((n, d), F32),
        grid=(nt,),
        in_specs=[
            pl.BlockSpec((TOP_K, T), lambda i: (0, i), memory_space=pltpu.SMEM),
            pl.BlockSpec((TOP_K, T), lambda i: (0, jnp.minimum(i + 1, nt - 1)),
                         memory_space=pltpu.SMEM),
            pl.BlockSpec((T, TOP_K), lambda i: (i, 0)),
            pl.BlockSpec(memory_space=pl.ANY),
            pl.BlockSpec((T, d2), lambda i: (i, 0)),
            pl.BlockSpec((T, d), lambda i: (i, 0)),
            pl.BlockSpec((1, d), lambda i: (0, 0)),
            pl.BlockSpec((1, d), lambda i: (0, 0)),
        ],
        out_specs=pl.BlockSpec((T, d), lambda i: (i, 0)),
        scratch_shapes=[
            pltpu.VMEM((2, TOP_K, T, d2), U32),
            pltpu.SemaphoreType.DMA((2,)),
        ],
        compiler_params=_cparams(("arbitrary",)),
        name="moe_combine",
    )(dest, dest, wk_t, ys, ysh, x, gate, gain)


def _moe(h2, x, gate, gain, final, layer, cast, cast_next, router_w, router_bias,
         w_gate, w_up, w_down, ws_gate, ws_up, ws_down):
    n, d2 = h2.shape
    e = router_w.shape[1]
    blk = EXPERT_BLOCK

    rw_t = router_w.T.astype(BF16)
    sel, wf, cnt = _router(h2, rw_t[:, :d2], rw_t[:, d2:], router_bias.reshape(e, 1))

    counts = jnp.sum(cnt, axis=1).astype(I32)
    padded = (counts + blk - 1) // blk * blk
    pad_end = jnp.cumsum(padded)
    pad_start = pad_end - padded
    n_blocks = (n * TOP_K + e * (blk - 1) + blk - 1) // blk
    block_start = jnp.arange(n_blocks, dtype=I32) * blk
    block_expert = jnp.minimum(
        jnp.sum((pad_end[None, :] <= block_start[:, None]).astype(I32), axis=1), e - 1)
    n_active = (pad_end[-1:] // blk).astype(I32)

    dest, wk = _rank(sel, wf, pad_start.astype(F32).reshape(e, 1))
    zero_len = (padded - counts).at[e - 1].add(n_blocks * blk - pad_end[-1])
    xs = _dispatch(h2, dest, pad_start + counts, zero_len, n_blocks * blk)

    wgu, wd = cast if cast is not None else _cast_experts(w_gate, w_up, w_down, layer)
    ys, next_cast = _grouped_ffn(xs, block_expert, n_active, wgu, wd,
                                 (w_gate, w_up, w_down, layer + 1) if cast_next else None)

    wsgu, wsd = _cast_experts(ws_gate[:, None], ws_up[:, None], ws_down[:, None], layer)
    ysh, _ = _grouped_ffn(h2, jnp.zeros((n // blk,), I32), jnp.full((1,), n // blk, I32),
                          wsgu, wsd)

    return _combine(dest, wk.T, ys, ysh, x, gate, gain, final), next_cast


def kernel(x, c, mod_w, mod_b, lru_w_in, lru_b_in, lru_conv_w, lru_conv_b, lru_gate_a_w,
           lru_gate_a_b, lru_gate_x_w, lru_gate_x_b, lru_lambda, lru_w_out, lru_b_out, pool_w,
           pool_b, pool_scale, router_w, router_bias, expert_w_gate, expert_w_up, expert_w_down,
           shared_w_gate, shared_w_up, shared_w_down, final_gain):
    bsz, seq, d = x.shape
    depth = mod_w.shape[0]
    assert bsz == 1, "one sequence per call"
    xf = x.reshape(seq, d)
    row = lambda v: v.reshape(1, -1)

    mod = _adaln_all(c, mod_w, mod_b)

    def adaln(i, s):
        m = mod[2 * i + s]
        return m[:, :d], m[:, d:2 * d], m[:, 2 * d:]

    gain = row(final_gain)
    experts = (expert_w_gate, expert_w_up, expert_w_down)
    cast = None
    for i in range(depth):
        j = i // 2
        shift, scale, gate = adaln(i, 0)
        shift2, scale2, gate2 = adaln(i, 1)
        if i % 2 == 0:
            h = _norm_mod(xf, scale, shift)
            g, u = _inproj(h, lru_w_in[j].astype(BF16), row(lru_b_in[j]))
            z, made = _lru_scan(u, g, lru_conv_w[j], row(lru_conv_b[j]),
                                lru_gate_a_w[j].astype(BF16), row(lru_gate_a_b[j]),
                                lru_gate_x_w[j].astype(BF16), row(lru_gate_x_b[j]),
                                row(lru_lambda[j]), (*experts, i) if cast is None else None)
            cast = made if cast is None else cast
            xf, h2 = _outproj(z, lru_w_out[j].astype(BF16), row(lru_b_out[j]), xf, gate,
                              scale2, shift2)
        else:
            xf, h2 = _pool_mixer(xf, scale, shift, pool_w[j].astype(BF16), row(pool_b[j]),
                                 row(pool_scale[j]), gate, scale2, shift2)
        xf, cast = _moe(h2, xf, gate2, gain, i == depth - 1, i, cast, i + 1 < depth,
                        router_w[i], router_bias[i], *experts,
                        shared_w_gate, shared_w_up, shared_w_down)
    return xf.reshape(bsz, seq, d)
```

```python
import functools

import jax
import jax.numpy as jnp
from jax import lax
from jax.experimental import pallas as pl
from jax.experimental.pallas import tpu as pltpu

F32 = jnp.float32
BF16 = jnp.bfloat16
U32 = jnp.uint32
I32 = jnp.int32

EPS = 1e-6
LRU_C = 8.0
N_GROUPS = 8
TOPK_GROUPS = 4
TOP_K = 8
ROUTED_SCALE = 2.5
POOL_WINDOWS = (2, 4, 8, 16)
EXPERT_BLOCK = 512
POOL_HIST = 24
VMEM_LIMIT = 56 * 1024 * 1024


def _cparams(sem, vmem=VMEM_LIMIT):
    return pltpu.CompilerParams(dimension_semantics=sem, vmem_limit_bytes=vmem)


def _pack_pair(lo, hi):
    lo_b = lax.bitcast_convert_type(lo.astype(BF16).astype(F32), U32)
    hi_b = lax.bitcast_convert_type(hi.astype(BF16).astype(F32), U32)
    return hi_b | (lo_b >> 16)


def _unpack_pair(w):
    lo = lax.bitcast_convert_type(w << 16, F32)
    hi = lax.bitcast_convert_type(w & jnp.uint32(0xFFFF0000), F32)
    return lo, hi


def _rms_inv(ss, d):
    return lax.rsqrt(ss / d + EPS)


CAST_SPLITS = (4, 2, 1)


def _cast_body(wg_ref, wu_ref, wd_ref, wgu_ref, wdo_ref):
    f = wg_ref.shape[-1]
    wgu_ref[:, :f] = wg_ref[...].astype(BF16)
    wgu_ref[:, f:] = wu_ref[...].astype(BF16)
    wdo_ref[...] = wd_ref[...].astype(BF16)


def _cast_plan(w_gate, layer, n_steps, step_fn):
    _, e, d, f = w_gate.shape
    q = next((s for s in CAST_SPLITS
              if e * s <= n_steps and f % s == 0 and d % s == 0 and (f // s) % 16 == 0), None)
    if q is None:
        return None
    dq, fq = d // q, f // q

    def piece(*a):
        u = jnp.minimum(step_fn(*a), e * q - 1)
        return u // q, u % q

    def w_in(*a):
        ex, pc = piece(*a)
        return layer, ex, pc, 0

    def w_out(*a):
        ex, pc = piece(*a)
        return ex, pc, 0

    in_specs = [pl.BlockSpec((None, None, dq, f), w_in), pl.BlockSpec((None, None, dq, f), w_in),
                pl.BlockSpec((None, None, fq, d), w_in)]
    out_specs = [pl.BlockSpec((None, dq, 2 * f), w_out), pl.BlockSpec((None, fq, d), w_out)]
    out_shapes = [jax.ShapeDtypeStruct((e, d, 2 * f), BF16), jax.ShapeDtypeStruct((e, f, d), BF16)]
    return in_specs, out_specs, out_shapes


def _cast_experts(w_gate, w_up, w_down, layer):
    e = w_gate.shape[1]
    n_steps = e * CAST_SPLITS[0]
    in_specs, out_specs, out_shapes = _cast_plan(w_gate, layer, n_steps, lambda i: i)
    return pl.pallas_call(
        _cast_body,
        out_shape=tuple(out_shapes),
        grid=(n_steps,),
        in_specs=in_specs,
        out_specs=tuple(out_specs),
        compiler_params=_cparams(("arbitrary",)),
        name="expert_cast",
    )(w_gate, w_up, w_down)


def _mod_kernel(c_ref, w_ref, b_ref, o_ref):
    c = c_ref[...]
    s = c * jax.nn.sigmoid(c)
    s8 = jnp.broadcast_to(s, (8, s.shape[1])).astype(BF16)
    acc = jnp.dot(s8, w_ref[...].astype(BF16), preferred_element_type=F32)
    o_ref[...] = acc[0:1, :] + b_ref[...]


def _adaln_all(c, mod_w, mod_b):
    depth, two, d, d3 = mod_w.shape
    na = depth * two
    tn = 512
    w = mod_w.reshape(na, d, d3)
    b = mod_b.reshape(na, 1, d3)
    return pl.pallas_call(
        _mod_kernel,
        out_shape=jax.ShapeDtypeStruct((na, 1, d3), F32),
        grid=(na, d3 // tn),
        in_specs=[
            pl.BlockSpec((1, d), lambda a, j: (0, 0)),
            pl.BlockSpec((None, d, tn), lambda a, j: (a, 0, j)),
            pl.BlockSpec((None, 1, tn), lambda a, j: (a, 0, j)),
        ],
        out_specs=pl.BlockSpec((None, 1, tn), lambda a, j: (a, 0, j)),
        compiler_params=_cparams(("arbitrary", "arbitrary")),
        name="adaln_mod",
    )(c, w, b)


def _norm_mod_kernel(x_ref, sc_ref, sh_ref, o_ref):
    x = x_ref[...]
    inv = _rms_inv(jnp.sum(x * x, axis=-1, keepdims=True), x.shape[-1])
    o_ref[...] = ((x * inv) * (1.0 + sc_ref[...]) + sh_ref[...]).astype(o_ref.dtype)


def _norm_mod(x, scale, shift):
    n, d = x.shape
    tm = min(512, n)
    return pl.pallas_call(
        _norm_mod_kernel,
        out_shape=jax.ShapeDtypeStruct((n, d), BF16),
        grid=(n // tm,),
        in_specs=[
            pl.BlockSpec((tm, d), lambda i: (i, 0)),
            pl.BlockSpec((1, d), lambda i: (0, 0)),
            pl.BlockSpec((1, d), lambda i: (0, 0)),
        ],
        out_specs=pl.BlockSpec((tm, d), lambda i: (i, 0)),
        compiler_params=_cparams(("arbitrary",)),
        name="norm_mod",
    )(x, scale, shift)


def _inproj_kernel(h_ref, wg_ref, wu_ref, bg_ref, bu_ref, g_ref, u_ref):
    h = h_ref[...]
    gate = jnp.dot(h, wg_ref[...], preferred_element_type=F32) + bg_ref[...]
    g_ref[...] = jax.nn.gelu(gate, approximate=True).astype(g_ref.dtype)
    u_ref[...] = jnp.dot(h, wu_ref[...], preferred_element_type=F32) + bu_ref[...]


def _inproj(h, w_in, b_in):
    n, d = h.shape
    dl = w_in.shape[1] // 2
    tm = min(1024, n)
    tn = min(512, dl)
    nj = dl // tn
    return pl.pallas_call(
        _inproj_kernel,
        out_shape=(jax.ShapeDtypeStruct((n, dl), BF16), jax.ShapeDtypeStruct((n, dl), F32)),
        grid=(n // tm, nj),
        in_specs=[
            pl.BlockSpec((tm, d), lambda i, j: (i, 0)),
            pl.BlockSpec((d, tn), lambda i, j: (0, j)),
            pl.BlockSpec((d, tn), lambda i, j: (0, j + nj)),
            pl.BlockSpec((1, tn), lambda i, j: (0, j)),
            pl.BlockSpec((1, tn), lambda i, j: (0, j + nj)),
        ],
        out_specs=(pl.BlockSpec((tm, tn), lambda i, j: (i, j)),
                   pl.BlockSpec((tm, tn), lambda i, j: (i, j))),
        compiler_params=_cparams(("arbitrary", "arbitrary")),
        name="lru_inproj",
    )(h, w_in, w_in, b_in, b_in)


def _lru_kernel(u_ref, g_ref, cw_ref, cb_ref, gaw_ref, gab_ref, gxw_ref, gxb_ref, lam_ref, *rest,
                T, CB, HD):
    uext, hcar, a_s, b_s = rest[-4:]
    z_ref = rest[-5] if len(rest) == 5 else rest[3]
    if len(rest) > 5:
        _cast_body(*rest[:3], *rest[4:6])
    t = pl.program_id(1)

    @pl.when(t == 0)
    def _():
        uext[0:8, :] = jnp.zeros((8, CB), F32)
        hcar[...] = jnp.zeros((8, CB), F32)

    uext[8:T + 8, :] = u_ref[...]
    cw = cw_ref[...]
    xc = (cb_ref[...] + cw[0:1, :] * uext[5:T + 5, :] + cw[1:2, :] * uext[6:T + 6, :]
          + cw[2:3, :] * uext[7:T + 7, :] + cw[3:4, :] * uext[8:T + 8, :])
    uext[0:8, :] = uext[T:T + 8, :]

    xcb = xc.astype(BF16)
    rs, xs = [], []
    for h in range(CB // HD):
        xh = xcb[:, h * HD:(h + 1) * HD]
        rs.append(jnp.dot(xh, gaw_ref[h], preferred_element_type=F32))
        xs.append(jnp.dot(xh, gxw_ref[h], preferred_element_type=F32))
    r = jax.nn.sigmoid(jnp.concatenate(rs, axis=1) + gab_ref[...])
    i = jax.nn.sigmoid(jnp.concatenate(xs, axis=1) + gxb_ref[...])
    lam = lam_ref[...]
    log_sig = jnp.minimum(lam, 0.0) - jnp.log(1.0 + jnp.exp(-jnp.abs(lam)))
    a = jnp.exp((LRU_C * r) * log_sig)
    b = jnp.sqrt(1.0 - a * a) * (i * xc)

    G = T // 8
    A = a.reshape(G, 8, CB)
    B = b.reshape(G, 8, CB)
    row = lax.broadcasted_iota(I32, (G, 8, CB), 1)
    for d in (1, 2, 4):
        keep = row >= d
        B = jnp.where(keep, A * pltpu.roll(B, d, axis=1) + B, B)
        A = jnp.where(keep, A * pltpu.roll(A, d, axis=1), A)
    a_s[...] = A.reshape(T, CB)
    b_s[...] = B.reshape(T, CB)

    def carry_step(gi, hprev):
        r0 = pl.multiple_of(gi * 8, 8)
        hrows = b_s[pl.ds(r0, 8), :] + a_s[pl.ds(r0, 8), :] * hprev
        b_s[pl.ds(r0, 8), :] = hrows
        return jnp.broadcast_to(hrows[7:8, :], (8, CB))

    hcar[...] = lax.fori_loop(0, G, carry_step, hcar[...])
    z_ref[...] = (b_s[...] * g_ref[...].astype(F32)).astype(z_ref.dtype)


def _lru_scan(u, g, conv_w, conv_b, ga_w, ga_b, gx_w, gx_b, lam, cast_src=None):
    n, dl = u.shape
    nh, hd, _ = ga_w.shape
    T = min(256, n)
    CB = min(1024, dl)
    hpb = CB // hd
    nt = n // T
    vec = lambda: pl.BlockSpec((1, CB), lambda c, t: (0, c))
    kern = functools.partial(_lru_kernel, T=T, CB=CB, HD=hd)
    in_specs = [
        pl.BlockSpec((T, CB), lambda c, t: (t, c)),
        pl.BlockSpec((T, CB), lambda c, t: (t, c)),
        pl.BlockSpec((conv_w.shape[0], CB), lambda c, t: (0, c)),
        vec(),
        pl.BlockSpec((hpb, hd, hd), lambda c, t: (c, 0, 0)),
        vec(),
        pl.BlockSpec((hpb, hd, hd), lambda c, t: (c, 0, 0)),
        vec(),
        vec(),
    ]
    out_specs = [pl.BlockSpec((T, CB), lambda c, t: (t, c))]
    out_shapes = [jax.ShapeDtypeStruct((n, dl), BF16)]
    args = [u, g, conv_w, conv_b, ga_w, ga_b, gx_w, gx_b, lam]
    plan = None
    if cast_src is not None:
        plan = _cast_plan(cast_src[0], cast_src[3], (dl // CB) * nt, lambda c, t: c * nt + t)
    if plan is not None:
        in_specs += plan[0]
        out_specs += plan[1]
        out_shapes += plan[2]
        args += list(cast_src[:3])
    outs = pl.pallas_call(
        kern,
        out_shape=tuple(out_shapes),
        grid=(dl // CB, nt),
        in_specs=in_specs,
        out_specs=tuple(out_specs),
        scratch_shapes=[
            pltpu.VMEM((T + 8, CB), F32),
            pltpu.VMEM((8, CB), F32),
            pltpu.VMEM((T, CB), F32),
            pltpu.VMEM((T, CB), F32),
        ],
        compiler_params=_cparams(("arbitrary", "arbitrary")),
        name="lru_scan",
    )(*args)
    return outs[0], (tuple(outs[1:]) if plan is not None else None)


def _norm_pack_store(acc, sc_ref, sh_ref, h_ref, nj, tn, d):
    ss = jnp.sum(acc[0] * acc[0], axis=-1, keepdims=True)
    for jj in range(1, nj):
        ss = ss + jnp.sum(acc[jj] * acc[jj], axis=-1, keepdims=True)
    inv = _rms_inv(ss, d)
    half = nj // 2
    for jj in range(half):
        lo_c = slice(jj * tn, (jj + 1) * tn)
        hi_c = slice((jj + half) * tn, (jj + half + 1) * tn)
        lo = (acc[jj] * inv) * (1.0 + sc_ref[:, lo_c]) + sh_ref[:, lo_c]
        hi = (acc[jj + half] * inv) * (1.0 + sc_ref[:, hi_c]) + sh_ref[:, hi_c]
        h_ref[:, lo_c] = _pack_pair(lo, hi)


def _outproj_kernel(z_ref, w_ref, b_ref, x_ref, gate_ref, sc_ref, sh_ref, x1_ref, h_ref, acc,
                    *, NJ, TN, D):
    j = pl.program_id(1)
    y = jnp.dot(z_ref[...], w_ref[...], preferred_element_type=F32) + b_ref[...]
    x1 = x_ref[...] + gate_ref[...] * y
    x1_ref[...] = x1
    acc[j] = x1

    @pl.when(j == NJ - 1)
    def _():
        _norm_pack_store(acc, sc_ref, sh_ref, h_ref, NJ, TN, D)


def _outproj(z, w_out, b_out, x, gate, scale, shift):
    n, dl = z.shape
    d = w_out.shape[1]
    tm = min(512, n)
    tn = min(1024, d // 2)
    nj = d // tn
    kern = functools.partial(_outproj_kernel, NJ=nj, TN=tn, D=d)
    return pl.pallas_call(
        kern,
        out_shape=(jax.ShapeDtypeStruct((n, d), F32), jax.ShapeDtypeStruct((n, d // 2), U32)),
        grid=(n // tm, nj),
        in_specs=[
            pl.BlockSpec((tm, dl), lambda i, j: (i, 0)),
            pl.BlockSpec((dl, tn), lambda i, j: (0, j)),
            pl.BlockSpec((1, tn), lambda i, j: (0, j)),
            pl.BlockSpec((tm, tn), lambda i, j: (i, j)),
            pl.BlockSpec((1, tn), lambda i, j: (0, j)),
            pl.BlockSpec((1, d), lambda i, j: (0, 0)),
            pl.BlockSpec((1, d), lambda i, j: (0, 0)),
        ],
        out_specs=(pl.BlockSpec((tm, tn), lambda i, j: (i, j)),
                   pl.BlockSpec((tm, d // 2), lambda i, j: (i, 0))),
        scratch_shapes=[pltpu.VMEM((nj, tm, tn), F32)],
        compiler_params=_cparams(("arbitrary", "arbitrary")),
        name="lru_outproj",
    )(z, w_out, b_out, x, gate, scale, shift)


def _pool_kernel(x_ref, sc_ref, sh_ref, pw_ref, pb_ref, ps_ref, gate_ref, sc2_ref, sh2_ref,
                 x3_ref, h_ref, ext, s_a, s_b, ybuf, *, T, D, GD):
    t = pl.program_id(0)
    H = POOL_HIST

    @pl.when(t == 0)
    def _():
        ext[0:H, :] = jnp.zeros((H, D), F32)
        s_a[0:8, :] = jnp.zeros((8, GD), F32)
        s_b[0:8, :] = jnp.zeros((8, GD), F32)

    x = x_ref[...]
    inv = _rms_inv(jnp.sum(x * x, axis=-1, keepdims=True), D)
    ext[H:T + H, :] = (x * inv) * (1.0 + sc_ref[...]) + sh_ref[...]

    pos1 = lax.broadcasted_iota(I32, (T, 1), 0) + (t * T + 1)
    for g, w in enumerate(POOL_WINDOWS):
        c = slice(g * GD, (g + 1) * GD)
        if w == 2:
            win = ext[H:T + H, c] + ext[H - 1:T + H - 1, c]
        else:
            s_a[8:T + H, :] = ext[8:T + H, c] + ext[7:T + H - 1, c]
            if w == 4:
                win = s_a[H:T + H, :] + s_a[H - 2:T + H - 2, :]
            else:
                s_b[8:T + H, :] = s_a[8:T + H, :] + s_a[6:T + H - 2, :]
                if w == 8:
                    win = s_b[H:T + H, :] + s_b[H - 4:T + H - 4, :]
                else:
                    s_a[8:T + H, :] = s_b[8:T + H, :] + s_b[4:T + H - 4, :]
                    win = s_a[H:T + H, :] + s_a[H - 8:T + H - 8, :]
        cnt = jnp.minimum(pos1, w).astype(F32)
        pooled = win / cnt - ext[H:T + H, c]
        yg = jnp.dot(pooled.astype(BF16), pw_ref[g], preferred_element_type=F32) + pb_ref[:, c]
        ybuf[:, c] = yg * ps_ref[:, c]
    ext[0:H, :] = ext[T:T + H, :]

    x3 = x + gate_ref[...] * ybuf[...]
    x3_ref[...] = x3
    inv3 = _rms_inv(jnp.sum(x3 * x3, axis=-1, keepdims=True), D)
    h4 = (x3 * inv3) * (1.0 + sc2_ref[...]) + sh2_ref[...]
    h_ref[...] = _pack_pair(h4[:, :D // 2], h4[:, D // 2:])


def _pool_mixer(x, scale, shift, pool_w, pool_b, pool_scale, gate, scale2, shift2):
    n, d = x.shape
    ng, gd, _ = pool_w.shape
    T = min(256, n)
    vec = lambda: pl.BlockSpec((1, d), lambda t: (0, 0))
    kern = functools.partial(_pool_kernel, T=T, D=d, GD=gd)
    return pl.pallas_call(
        kern,
        out_shape=(jax.ShapeDtypeStruct((n, d), F32), jax.ShapeDtypeStruct((n, d // 2), U32)),
        grid=(n // T,),
        in_specs=[
            pl.BlockSpec((T, d), lambda t: (t, 0)),
            vec(), vec(),
            pl.BlockSpec((ng, gd, gd), lambda t: (0, 0, 0)),
            vec(), vec(), vec(), vec(), vec(),
        ],
        out_specs=(pl.BlockSpec((T, d), lambda t: (t, 0)),
                   pl.BlockSpec((T, d // 2), lambda t: (t, 0))),
        scratch_shapes=[
            pltpu.VMEM((T + POOL_HIST, d), F32),
            pltpu.VMEM((T + POOL_HIST, gd), F32),
            pltpu.VMEM((T + POOL_HIST, gd), F32),
            pltpu.VMEM((T, d), F32),
        ],
        compiler_params=_cparams(("arbitrary",)),
        name="pool_mixer",
    )(x, scale, shift, pool_w, pool_b, pool_scale, gate, scale2, shift2)


def _router_kernel(h_ref, wlo_ref, whi_ref, rb_ref, sel_ref, wf_ref, cnt_ref, *, T, E):
    i = pl.program_id(0)
    NG = N_GROUPS
    GS = E // NG
    lo, hi = _unpack_pair(h_ref[...])
    dn = (((1,), (1,)), ((), ()))
    logits = (lax.dot_general(wlo_ref[...], lo.astype(BF16), dn, preferred_element_type=F32)
              + lax.dot_general(whi_ref[...], hi.astype(BF16), dn, preferred_element_type=F32))
    scores = jax.nn.sigmoid(logits).reshape(NG, GS, T)
    biased = scores + rb_ref[...].reshape(NG, GS, 1)

    sub = lax.broadcasted_iota(I32, (NG, GS, T), 1)
    m1 = jnp.max(biased, axis=1, keepdims=True)
    first = jnp.min(jnp.where(biased == m1, sub, GS), axis=1, keepdims=True)
    m2 = jnp.max(jnp.where(sub == first, -jnp.inf, biased), axis=1, keepdims=True)
    gscore = m1 + m2

    gidx = lax.broadcasted_iota(I32, (NG, 1, T), 0)
    beaten = jnp.zeros((NG, 1, T), F32)
    for j in range(NG):
        sj = gscore[j:j + 1]
        beaten = beaten + jnp.where(sj > gscore, 1.0,
                                    jnp.where(sj == gscore, jnp.where(gidx > j, 1.0, 0.0), 0.0))
    masked = jnp.where(beaten < TOPK_GROUPS, biased, -jnp.inf)

    eidx = lax.broadcasted_iota(I32, (NG, GS, T), 0) * GS + sub
    beaten = jnp.zeros((NG, GS, T), F32)
    for g in range(NG):
        mg = masked[g]
        for s in range(GS):
            v = mg[s:s + 1, :][None]
            later = jnp.where(eidx > g * GS + s, 1.0, 0.0)
            beaten = beaten + jnp.where(v > masked, 1.0, jnp.where(v == masked, later, 0.0))
    sel = jnp.where(beaten < TOP_K, 1.0, 0.0)

    picked = sel * scores
    tot = jnp.sum(jnp.sum(picked, axis=1, keepdims=True), axis=0, keepdims=True)
    wf = picked / tot * ROUTED_SCALE

    sel2 = sel.reshape(E, T)
    sel_ref[...] = sel2
    wf_ref[...] = wf.reshape(E, T)

    @pl.when(i == 0)
    def _():
        cnt_ref[...] = jnp.zeros_like(cnt_ref)
    part = sel2[:, 0:128]
    for c in range(1, T // 128):
        part = part + sel2[:, c * 128:(c + 1) * 128]
    cnt_ref[...] += part


def _router(h2, wlo, whi, rbias):
    n, d2 = h2.shape
    e = wlo.shape[0]
    T = min(512, n)
    kern = functools.partial(_router_kernel, T=T, E=e)
    return pl.pallas_call(
        kern,
        out_shape=(jax.ShapeDtypeStruct((e, n), F32), jax.ShapeDtypeStruct((e, n), F32),
                   jax.ShapeDtypeStruct((e, 128), F32)),
        grid=(n // T,),
        in_specs=[
            pl.BlockSpec((T, d2), lambda i: (i, 0)),
            pl.BlockSpec((e, d2), lambda i: (0, 0)),
            pl.BlockSpec((e, d2), lambda i: (0, 0)),
            pl.BlockSpec((e, 1), lambda i: (0, 0)),
        ],
        out_specs=(pl.BlockSpec((e, T), lambda i: (0, i)),
                   pl.BlockSpec((e, T), lambda i: (0, i)),
                   pl.BlockSpec((e, 128), lambda i: (0, 0))),
        compiler_params=_cparams(("arbitrary",)),
        name="moe_router",
    )(h2, wlo, whi, rbias)


def _rank_kernel(sel_ref, wf_ref, ps_ref, dest_ref, wk_ref, carry, *, T, E):
    i = pl.program_id(0)

    @pl.when(i == 0)
    def _():
        carry[...] = jnp.zeros_like(carry)

    sel = sel_ref[...]
    selb = sel.astype(BF16)
    before = jnp.where(lax.broadcasted_iota(I32, (T, T), 0) < lax.broadcasted_iota(I32, (T, T), 1),
                       1.0, 0.0).astype(BF16)
    rank = jnp.dot(selb, before, preferred_element_type=F32)
    destf = ps_ref[...] + carry[...] + rank
    lower = jnp.where(lax.broadcasted_iota(I32, (E, E), 1) < lax.broadcasted_iota(I32, (E, E), 0),
                      1.0, 0.0).astype(BF16)
    slot = jnp.dot(lower, selb, preferred_element_type=F32)
    wf = wf_ref[...]
    for k in range(TOP_K):
        mk = jnp.where(slot == k, sel, 0.0)
        dest_ref[k:k + 1, :] = jnp.sum(mk * destf, axis=0, keepdims=True).astype(I32)
        wk_ref[k:k + 1, :] = jnp.sum(mk * wf, axis=0, keepdims=True)
    carry[...] += jnp.sum(sel, axis=1, keepdims=True)


def _rank(sel, wf, pad_start):
    e, n = sel.shape
    T = min(512, n)
    kern = functools.partial(_rank_kernel, T=T, E=e)
    return pl.pallas_call(
        kern,
        out_shape=(jax.ShapeDtypeStruct((TOP_K, n), I32), jax.ShapeDtypeStruct((TOP_K, n), F32)),
        grid=(n // T,),
        in_specs=[
            pl.BlockSpec((e, T), lambda i: (0, i)),
            pl.BlockSpec((e, T), lambda i: (0, i)),
            pl.BlockSpec((e, 1), lambda i: (0, 0)),
        ],
        out_specs=(pl.BlockSpec((TOP_K, T), lambda i: (0, i)),
                   pl.BlockSpec((TOP_K, T), lambda i: (0, i))),
        scratch_shapes=[pltpu.VMEM((e, 1), F32)],
        compiler_params=_cparams(("arbitrary",)),
        name="moe_rank",
    )(sel, wf, pad_start)


FFN_CHAIN_ROWS = 128


def _ffn_chain(x, wgu_ref, wd_ref):
    d2 = x.shape[1]
    f = wd_ref.shape[0]
    lo, hi = _unpack_pair(x)
    gu = (jnp.dot(lo.astype(BF16), wgu_ref[0:d2, :], preferred_element_type=F32)
          + jnp.dot(hi.astype(BF16), wgu_ref[d2:2 * d2, :], preferred_element_type=F32))
    act = jax.nn.silu(gu[:, :f]) * gu[:, f:]
    y = jnp.dot(act.astype(BF16), wd_ref[...], preferred_element_type=F32)
    return _pack_pair(y[:, :d2], y[:, d2:])


def _dispatch_kernel(zs_ref, zl_ref, dest_ref, h_ref, wsgu_ref, wsd_ref, xs_ref, ysh_ref,
                     zbuf, sem, zsem, *, T, E):
    i = pl.program_id(0)

    def row_copy(t, k):
        return pltpu.make_async_copy(h_ref.at[pl.ds(t, 1), :],
                                     xs_ref.at[pl.ds(dest_ref[k, t], 1), :], sem)

    def for_each_pad_piece(fn):
        def per_expert(e, _):
            start = zs_ref[e]
            ln = zl_ref[e]
            head = jnp.minimum((8 - (start & 7)) & 7, ln)

            def per_row(r, _):
                fn(pltpu.make_async_copy(zbuf.at[pl.ds(0, 1), :],
                                         xs_ref.at[pl.ds(start + r, 1), :], zsem))
                return 0
            lax.fori_loop(0, head, per_row, 0)

            def per_group(q, _):
                r0 = pl.multiple_of(start + head + q * 8, 8)
                fn(pltpu.make_async_copy(zbuf, xs_ref.at[pl.ds(r0, 8), :], zsem))
                return 0
            lax.fori_loop(0, (ln - head) // 8, per_group, 0)
            return 0
        lax.fori_loop(0, E, per_expert, 0)

    @pl.when(i == 0)
    def _():
        zbuf[...] = jnp.zeros_like(zbuf)
        for_each_pad_piece(lambda cp: cp.start())

    def issue(t, _):
        for k in range(TOP_K):
            row_copy(t, k).start()
        return 0
    lax.fori_loop(0, T, issue, 0)

    for s in range(T // FFN_CHAIN_ROWS):
        rs = slice(s * FFN_CHAIN_ROWS, (s + 1) * FFN_CHAIN_ROWS)
        ysh_ref[rs, :] = _ffn_chain(h_ref[rs, :], wsgu_ref, wsd_ref)

    for k in range(TOP_K):
        pltpu.make_async_copy(h_ref, xs_ref.at[pl.ds(0, T), :], sem).wait()

    @pl.when(i == 0)
    def _():
        for_each_pad_piece(lambda cp: cp.wait())


def _dispatch(h2, dest, zero_start, zero_len, n_rows, wsgu, wsd):
    n, d2 = h2.shape
    e = zero_start.shape[0]
    _, d, f2 = wsgu.shape
    T = min(256, n)
    kern = functools.partial(_dispatch_kernel, T=T, E=e)
    grid_spec = pltpu.PrefetchScalarGridSpec(
        num_scalar_prefetch=2,
        grid=(n // T,),
        in_specs=[
            pl.BlockSpec((TOP_K, T), lambda i, zs, zl: (0, i), memory_space=pltpu.SMEM),
            pl.BlockSpec((T, d2), lambda i, zs, zl: (i, 0)),
            pl.BlockSpec((None, d, f2), lambda i, zs, zl: (0, 0, 0)),
            pl.BlockSpec((None, f2 // 2, d), lambda i, zs, zl: (0, 0, 0)),
        ],
        out_specs=(pl.BlockSpec(memory_space=pl.ANY),
                   pl.BlockSpec((T, d2), lambda i, zs, zl: (i, 0))),
        scratch_shapes=[
            pltpu.VMEM((8, d2), U32),
            pltpu.SemaphoreType.DMA(()),
            pltpu.SemaphoreType.DMA(()),
        ],
    )
    return pl.pallas_call(
        kern,
        out_shape=(jax.ShapeDtypeStruct((n_rows, d2), U32), jax.ShapeDtypeStruct((n, d2), U32)),
        grid_spec=grid_spec,
        compiler_params=_cparams(("arbitrary",)),
        name="moe_dispatch",
    )(zero_start, zero_len, dest, h2, wsgu, wsd)


def _ffn_kernel(be_ref, na_ref, nc_ref, x_ref, wgu_ref, wd_ref, *rest):
    y_ref = rest[-3] if len(rest) > 1 else rest[0]
    if len(rest) > 1:
        _cast_body(*rest[:3], *rest[-2:])
    n_chains = x_ref.shape[0] // FFN_CHAIN_ROWS
    used = nc_ref[pl.program_id(0)]

    def chain(s):
        rs = slice(s * FFN_CHAIN_ROWS, (s + 1) * FFN_CHAIN_ROWS)
        y_ref[rs, :] = _ffn_chain(x_ref[rs, :], wgu_ref, wd_ref)

    @pl.when(used == n_chains)
    def _():
        for s in range(n_chains):
            chain(s)

    @pl.when(used < n_chains)
    def _():
        for s in range(n_chains):
            @pl.when(s < used)
            def _():
                chain(s)

            @pl.when(s >= used)
            def _():
                rs = slice(s * FFN_CHAIN_ROWS, (s + 1) * FFN_CHAIN_ROWS)
                y_ref[rs, :] = jnp.zeros((FFN_CHAIN_ROWS, y_ref.shape[1]), y_ref.dtype)


def _grouped_ffn(xs, block_expert, n_active, n_chains, wgu, wd, cast_src=None):
    rows, d2 = xs.shape
    e, d, f2 = wgu.shape
    f = f2 // 2
    nb = rows // EXPERT_BLOCK

    def blk(b, be, na, nc):
        return jnp.minimum(b, na[0] - 1)

    in_specs = [
        pl.BlockSpec((EXPERT_BLOCK, d2), lambda b, be, na, nc: (blk(b, be, na, nc), 0)),
        pl.BlockSpec((None, d, f2), lambda b, be, na, nc: (be[blk(b, be, na, nc)], 0, 0)),
        pl.BlockSpec((None, f, d), lambda b, be, na, nc: (be[blk(b, be, na, nc)], 0, 0)),
    ]
    out_specs = [pl.BlockSpec((EXPERT_BLOCK, d2), lambda b, be, na, nc: (b, 0))]
    out_shapes = [jax.ShapeDtypeStruct((rows, d2), U32)]
    args = [block_expert, n_active, n_chains, xs, wgu, wd]
    plan = None
    if cast_src is not None:
        plan = _cast_plan(cast_src[0], cast_src[3], nb, lambda b, be, na, nc: b)
    if plan is not None:
        in_specs += plan[0]
        out_specs += plan[1]
        out_shapes += plan[2]
        args += list(cast_src[:3])
    grid_spec = pltpu.PrefetchScalarGridSpec(
        num_scalar_prefetch=3, grid=(nb,), in_specs=in_specs, out_specs=tuple(out_specs))
    outs = pl.pallas_call(
        _ffn_kernel,
        out_shape=tuple(out_shapes),
        grid_spec=grid_spec,
        compiler_params=_cparams(("arbitrary",)),
        name="moe_ffn",
    )(*args)
    return outs[0], (tuple(outs[1:]) if plan is not None else None)


def _combine_kernel(dcur_ref, dnext_ref, wk_ref, ys_ref, ysh_ref, x_ref, gate_ref, gain_ref, o_ref,
                    buf, sems, *, T, D, NT, CW, FINAL):
    i = pl.program_id(0)
    slot = lax.rem(i, 2)
    D2 = D // 2

    def issue(dref, sl, t):
        for k in range(TOP_K):
            pltpu.make_async_copy(ys_ref.at[pl.ds(dref[k, t], 1), :],
                                  buf.at[sl, k, pl.ds(t, 1), :], sems.at[sl]).start()

    def wait_slot(sl):
        for k in range(TOP_K):
            pltpu.make_async_copy(ys_ref.at[pl.ds(0, T), :], buf.at[sl, k], sems.at[sl]).wait()

    @pl.when(i == 0)
    def _():
        def body(t, _):
            issue(dcur_ref, 0, t)
            return 0
        lax.fori_loop(0, T, body, 0)

    wait_slot(slot)

    def row_group(rg, _):
        for tt in range(8):
            issue(dnext_ref, 1 - slot, rg * 8 + tt)
        r = pl.ds(pl.multiple_of(rg * 8, 8), 8)
        w8 = wk_ref[r, :]
        wk = [w8[:, k:k + 1] for k in range(TOP_K)]
        ss = jnp.zeros((8, 1), F32)
        for c in range(D2 // CW):
            c_lo = slice(c * CW, (c + 1) * CW)
            c_hi = slice(D2 + c * CW, D2 + (c + 1) * CW)
            lo, hi = _unpack_pair(ysh_ref[r, c_lo])
            for k in range(TOP_K):
                lo_k, hi_k = _unpack_pair(buf[slot, k, r, c_lo])
                lo = lo + wk[k] * lo_k
                hi = hi + wk[k] * hi_k
            xo_lo = x_ref[r, c_lo] + gate_ref[:, c_lo] * lo
            xo_hi = x_ref[r, c_hi] + gate_ref[:, c_hi] * hi
            if FINAL:
                ss = ss + (jnp.sum(xo_lo * xo_lo, axis=-1, keepdims=True)
                           + jnp.sum(xo_hi * xo_hi, axis=-1, keepdims=True))
            o_ref[r, c_lo] = xo_lo
            o_ref[r, c_hi] = xo_hi
        if FINAL:
            inv = _rms_inv(ss, D)
            for c in range(D // CW):
                cs = slice(c * CW, (c + 1) * CW)
                o_ref[r, cs] = (o_ref[r, cs] * inv) * gain_ref[:, cs]
        return 0
    lax.fori_loop(0, T // 8, row_group, 0, unroll=4)

    @pl.when(i == NT - 1)
    def _():
        wait_slot(1 - slot)


def _combine(dest, wk_t, ys, ysh, x, gate, gain, final):
    n, d = x.shape
    d2 = d // 2
    T = min(128, n)
    nt = n // T
    kern = functools.partial(_combine_kernel, T=T, D=d, NT=nt, CW=min(512, d2), FINAL=final)
    return pl.pallas_call(
        kern,
        out_shape=jax.ShapeDtypeStruct((n, d), F32),
        grid=(nt,),
        in_specs=[
            pl.BlockSpec((TOP_K, T), lambda i: (0, i), memory_space=pltpu.SMEM),
            pl.BlockSpec((TOP_K, T), lambda i: (0, jnp.minimum(i + 1, nt - 1)),
                         memory_space=pltpu.SMEM),
            pl.BlockSpec((T, TOP_K), lambda i: (i, 0)),
            pl.BlockSpec(memory_space=pl.ANY),
            pl.BlockSpec((T, d2), lambda i: (i, 0)),
            pl.BlockSpec((T, d), lambda i: (i, 0)),
            pl.BlockSpec((1, d), lambda i: (0, 0)),
            pl.BlockSpec((1, d), lambda i: (0, 0)),
        ],
        out_specs=pl.BlockSpec((T, d), lambda i: (i, 0)),
        scratch_shapes=[
            pltpu.VMEM((2, TOP_K, T, d2), U32),
            pltpu.SemaphoreType.DMA((2,)),
        ],
        compiler_params=_cparams(("arbitrary",)),
        name="moe_combine",
    )(dest, dest, wk_t, ys, ysh, x, gate, gain)


def _moe(h2, x, gate, gain, final, layer, cast, cast_next, router_w, router_bias,
         w_gate, w_up, w_down, ws_gate, ws_up, ws_down):
    n, d2 = h2.shape
    e = router_w.shape[1]
    blk = EXPERT_BLOCK

    rw_t = router_w.T.astype(BF16)
    sel, wf, cnt = _router(h2, rw_t[:, :d2], rw_t[:, d2:], router_bias.reshape(e, 1))

    counts = jnp.sum(cnt, axis=1).astype(I32)
    padded = (counts + blk - 1) // blk * blk
    pad_end = jnp.cumsum(padded)
    pad_start = pad_end - padded
    n_blocks = (n * TOP_K + e * (blk - 1) + blk - 1) // blk
    block_start = jnp.arange(n_blocks, dtype=I32) * blk
    block_expert = jnp.minimum(
        jnp.sum((pad_end[None, :] <= block_start[:, None]).astype(I32), axis=1), e - 1)
    n_active = (pad_end[-1:] // blk).astype(I32)

    rows_left = counts[block_expert] - (block_start - pad_start[block_expert])
    n_chains = jnp.where(block_start < pad_end[-1],
                         (jnp.clip(rows_left, 0, blk) + FFN_CHAIN_ROWS - 1) // FFN_CHAIN_ROWS, 0)

    dest, wk = _rank(sel, wf, pad_start.astype(F32).reshape(e, 1))
    zero_len = (padded - counts).at[e - 1].add(n_blocks * blk - pad_end[-1])
    wsgu, wsd = _cast_experts(ws_gate[:, None], ws_up[:, None], ws_down[:, None], layer)
    xs, ysh = _dispatch(h2, dest, pad_start + counts, zero_len, n_blocks * blk, wsgu, wsd)

    wgu, wd = cast if cast is not None else _cast_experts(w_gate, w_up, w_down, layer)
    ys, next_cast = _grouped_ffn(xs, block_expert, n_active, n_chains.astype(I32), wgu, wd,
                                 (w_gate, w_up, w_down, layer + 1) if cast_next else None)

    return _combine(dest, wk.T, ys, ysh, x, gate, gain, final), next_cast


def kernel(x, c, mod_w, mod_b, lru_w_in, lru_b_in, lru_conv_w, lru_conv_b, lru_gate_a_w,
           lru_gate_a_b, lru_gate_x_w, lru_gate_x_b, lru_lambda, lru_w_out, lru_b_out, pool_w,
           pool_b, pool_scale, router_w, router_bias, expert_w_gate, expert_w_up, expert_w_down,
           shared_w_gate, shared_w_up, shared_w_down, final_gain):
    bsz, seq, d = x.shape
    depth = mod_w.shape[0]
    assert bsz == 1, "one sequence per call"
    xf = x.reshape(seq, d)
    row = lambda v: v.reshape(1, -1)

    mod = _adaln_all(c, mod_w, mod_b)

    def adaln(i, s):
        m = mod[2 * i + s]
        return m[:, :d], m[:, d:2 * d], m[:, 2 * d:]

    gain = row(final_gain)
    experts = (expert_w_gate, expert_w_up, expert_w_down)
    cast = None
    for i in range(depth):
        j = i // 2
        shift, scale, gate = adaln(i, 0)
        shift2, scale2, gate2 = adaln(i, 1)
        if i % 2 == 0:
            h = _norm_mod(xf, scale, shift)
            g, u = _inproj(h, lru_w_in[j].astype(BF16), row(lru_b_in[j]))
            z, made = _lru_scan(u, g, lru_conv_w[j], row(lru_conv_b[j]),
                                lru_gate_a_w[j].astype(BF16), row(lru_gate_a_b[j]),
                                lru_gate_x_w[j].astype(BF16), row(lru_gate_x_b[j]),
                                row(lru_lambda[j]), (*experts, i) if cast is None else None)
            cast = made if cast is None else cast
            xf, h2 = _outproj(z, lru_w_out[j].astype(BF16), row(lru_b_out[j]), xf, gate,
                              scale2, shift2)
        else:
            xf, h2 = _pool_mixer(xf, scale, shift, pool_w[j].astype(BF16), row(pool_b[j]),
                                 row(pool_scale[j]), gate, scale2, shift2)
        xf, cast = _moe(h2, xf, gate2, gain, i == depth - 1, i, cast, i + 1 < depth,
                        router_w[i], router_bias[i], *experts,
                        shared_w_gate, shared_w_up, shared_w_down)
    return xf.reshape(bsz, seq, d)
```

```python
import functools

import jax
import jax.numpy as jnp
from jax import lax
from jax.experimental import pallas as pl
from jax.experimental.pallas import tpu as pltpu

F32 = jnp.float32
BF16 = jnp.bfloat16
U32 = jnp.uint32
I32 = jnp.int32

EPS = 1e-6
LRU_C = 8.0
N_GROUPS = 8
TOPK_GROUPS = 4
TOP_K = 8
ROUTED_SCALE = 2.5
POOL_WINDOWS = (2, 4, 8, 16)
EXPERT_BLOCK = 512
POOL_HIST = 24
VMEM_LIMIT = 56 * 1024 * 1024


def _cparams(sem, vmem=VMEM_LIMIT):
    return pltpu.CompilerParams(dimension_semantics=sem, vmem_limit_bytes=vmem)


def _pack_pair(lo, hi):
    lo_b = lax.bitcast_convert_type(lo.astype(BF16).astype(F32), U32)
    hi_b = lax.bitcast_convert_type(hi.astype(BF16).astype(F32), U32)
    return hi_b | (lo_b >> 16)


def _unpack_pair(w):
    lo = lax.bitcast_convert_type(w << 16, F32)
    hi = lax.bitcast_convert_type(w & jnp.uint32(0xFFFF0000), F32)
    return lo, hi


def _rms_inv(ss, d):
    return lax.rsqrt(ss / d + EPS)


CAST_SPLITS = (4, 2, 1)


def _cast_body(wg_ref, wu_ref, wd_ref, wgu_ref, wdo_ref):
    f = wg_ref.shape[-1]
    wgu_ref[:, :f] = wg_ref[...].astype(BF16)
    wgu_ref[:, f:] = wu_ref[...].astype(BF16)
    wdo_ref[...] = wd_ref[...].astype(BF16)


def _cast_plan(w_gate, layer, n_steps, step_fn):
    _, e, d, f = w_gate.shape
    q = next((s for s in CAST_SPLITS
              if e * s <= n_steps and f % s == 0 and d % s == 0 and (f // s) % 16 == 0), None)
    if q is None:
        return None
    dq, fq = d // q, f // q

    def piece(*a):
        u = jnp.minimum(step_fn(*a), e * q - 1)
        return u // q, u % q

    def w_in(*a):
        ex, pc = piece(*a)
        return layer, ex, pc, 0

    def w_out(*a):
        ex, pc = piece(*a)
        return ex, pc, 0

    in_specs = [pl.BlockSpec((None, None, dq, f), w_in), pl.BlockSpec((None, None, dq, f), w_in),
                pl.BlockSpec((None, None, fq, d), w_in)]
    out_specs = [pl.BlockSpec((None, dq, 2 * f), w_out), pl.BlockSpec((None, fq, d), w_out)]
    out_shapes = [jax.ShapeDtypeStruct((e, d, 2 * f), BF16), jax.ShapeDtypeStruct((e, f, d), BF16)]
    return in_specs, out_specs, out_shapes


def _cast_experts(w_gate, w_up, w_down, layer):
    e = w_gate.shape[1]
    n_steps = e * CAST_SPLITS[0]
    in_specs, out_specs, out_shapes = _cast_plan(w_gate, layer, n_steps, lambda i: i)
    return pl.pallas_call(
        _cast_body,
        out_shape=tuple(out_shapes),
        grid=(n_steps,),
        in_specs=in_specs,
        out_specs=tuple(out_specs),
        compiler_params=_cparams(("arbitrary",)),
        name="expert_cast",
    )(w_gate, w_up, w_down)


def _mod_kernel(c_ref, w_ref, b_ref, o_ref):
    c = c_ref[...]
    s = c * jax.nn.sigmoid(c)
    s8 = jnp.broadcast_to(s, (8, s.shape[1])).astype(BF16)
    acc = jnp.dot(s8, w_ref[...].astype(BF16), preferred_element_type=F32)
    o_ref[...] = acc[0:1, :] + b_ref[...]


def _adaln_all(c, mod_w, mod_b):
    depth, two, d, d3 = mod_w.shape
    na = depth * two
    tn = 512
    w = mod_w.reshape(na, d, d3)
    b = mod_b.reshape(na, 1, d3)
    return pl.pallas_call(
        _mod_kernel,
        out_shape=jax.ShapeDtypeStruct((na, 1, d3), F32),
        grid=(na, d3 // tn),
        in_specs=[
            pl.BlockSpec((1, d), lambda a, j: (0, 0)),
            pl.BlockSpec((None, d, tn), lambda a, j: (a, 0, j)),
            pl.BlockSpec((None, 1, tn), lambda a, j: (a, 0, j)),
        ],
        out_specs=pl.BlockSpec((None, 1, tn), lambda a, j: (a, 0, j)),
        compiler_params=_cparams(("arbitrary", "arbitrary")),
        name="adaln_mod",
    )(c, w, b)


def _norm_mod_kernel(x_ref, sc_ref, sh_ref, o_ref):
    x = x_ref[...]
    inv = _rms_inv(jnp.sum(x * x, axis=-1, keepdims=True), x.shape[-1])
    o_ref[...] = ((x * inv) * (1.0 + sc_ref[...]) + sh_ref[...]).astype(o_ref.dtype)


def _norm_mod(x, scale, shift):
    n, d = x.shape
    tm = min(512, n)
    return pl.pallas_call(
        _norm_mod_kernel,
        out_shape=jax.ShapeDtypeStruct((n, d), BF16),
        grid=(n // tm,),
        in_specs=[
            pl.BlockSpec((tm, d), lambda i: (i, 0)),
            pl.BlockSpec((1, d), lambda i: (0, 0)),
            pl.BlockSpec((1, d), lambda i: (0, 0)),
        ],
        out_specs=pl.BlockSpec((tm, d), lambda i: (i, 0)),
        compiler_params=_cparams(("arbitrary",)),
        name="norm_mod",
    )(x, scale, shift)


def _inproj_kernel(h_ref, wg_ref, wu_ref, bg_ref, bu_ref, g_ref, u_ref):
    h = h_ref[...]
    gate = jnp.dot(h, wg_ref[...], preferred_element_type=F32) + bg_ref[...]
    g_ref[...] = jax.nn.gelu(gate, approximate=True).astype(g_ref.dtype)
    u_ref[...] = jnp.dot(h, wu_ref[...], preferred_element_type=F32) + bu_ref[...]


def _inproj(h, w_in, b_in):
    n, d = h.shape
    dl = w_in.shape[1] // 2
    tm = min(1024, n)
    tn = min(512, dl)
    nj = dl // tn
    return pl.pallas_call(
        _inproj_kernel,
        out_shape=(jax.ShapeDtypeStruct((n, dl), BF16), jax.ShapeDtypeStruct((n, dl), F32)),
        grid=(n // tm, nj),
        in_specs=[
            pl.BlockSpec((tm, d), lambda i, j: (i, 0)),
            pl.BlockSpec((d, tn), lambda i, j: (0, j)),
            pl.BlockSpec((d, tn), lambda i, j: (0, j + nj)),
            pl.BlockSpec((1, tn), lambda i, j: (0, j)),
            pl.BlockSpec((1, tn), lambda i, j: (0, j + nj)),
        ],
        out_specs=(pl.BlockSpec((tm, tn), lambda i, j: (i, j)),
                   pl.BlockSpec((tm, tn), lambda i, j: (i, j))),
        compiler_params=_cparams(("arbitrary", "arbitrary")),
        name="lru_inproj",
    )(h, w_in, w_in, b_in, b_in)


def _lru_kernel(u_ref, g_ref, cw_ref, cb_ref, gaw_ref, gab_ref, gxw_ref, gxb_ref, lam_ref, *rest,
                T, CB, HD):
    uext, hcar, a_s, b_s = rest[-4:]
    z_ref = rest[-5] if len(rest) == 5 else rest[3]
    if len(rest) > 5:
        _cast_body(*rest[:3], *rest[4:6])
    t = pl.program_id(1)

    @pl.when(t == 0)
    def _():
        uext[0:8, :] = jnp.zeros((8, CB), F32)
        hcar[...] = jnp.zeros((8, CB), F32)

    uext[8:T + 8, :] = u_ref[...]
    cw = cw_ref[...]
    xc = (cb_ref[...] + cw[0:1, :] * uext[5:T + 5, :] + cw[1:2, :] * uext[6:T + 6, :]
          + cw[2:3, :] * uext[7:T + 7, :] + cw[3:4, :] * uext[8:T + 8, :])
    uext[0:8, :] = uext[T:T + 8, :]

    xcb = xc.astype(BF16)
    rs, xs = [], []
    for h in range(CB // HD):
        xh = xcb[:, h * HD:(h + 1) * HD]
        rs.append(jnp.dot(xh, gaw_ref[h], preferred_element_type=F32))
        xs.append(jnp.dot(xh, gxw_ref[h], preferred_element_type=F32))
    r = jax.nn.sigmoid(jnp.concatenate(rs, axis=1) + gab_ref[...])
    i = jax.nn.sigmoid(jnp.concatenate(xs, axis=1) + gxb_ref[...])
    lam = lam_ref[...]
    log_sig = jnp.minimum(lam, 0.0) - jnp.log(1.0 + jnp.exp(-jnp.abs(lam)))
    a = jnp.exp((LRU_C * r) * log_sig)
    b = jnp.sqrt(1.0 - a * a) * (i * xc)

    G = T // 8
    A = a.reshape(G, 8, CB)
    B = b.reshape(G, 8, CB)
    row = lax.broadcasted_iota(I32, (G, 8, CB), 1)
    for d in (1, 2, 4):
        keep = row >= d
        B = jnp.where(keep, A * pltpu.roll(B, d, axis=1) + B, B)
        A = jnp.where(keep, A * pltpu.roll(A, d, axis=1), A)
    a_s[...] = A.reshape(T, CB)
    b_s[...] = B.reshape(T, CB)

    def carry_step(gi, hprev):
        r0 = pl.multiple_of(gi * 8, 8)
        hrows = b_s[pl.ds(r0, 8), :] + a_s[pl.ds(r0, 8), :] * hprev
        b_s[pl.ds(r0, 8), :] = hrows
        return jnp.broadcast_to(hrows[7:8, :], (8, CB))

    hcar[...] = lax.fori_loop(0, G, carry_step, hcar[...])
    z_ref[...] = (b_s[...] * g_ref[...].astype(F32)).astype(z_ref.dtype)


def _lru_scan(u, g, conv_w, conv_b, ga_w, ga_b, gx_w, gx_b, lam, cast_src=None):
    n, dl = u.shape
    nh, hd, _ = ga_w.shape
    T = min(256, n)
    CB = min(1024, dl)
    hpb = CB // hd
    nt = n // T
    vec = lambda: pl.BlockSpec((1, CB), lambda c, t: (0, c))
    kern = functools.partial(_lru_kernel, T=T, CB=CB, HD=hd)
    in_specs = [
        pl.BlockSpec((T, CB), lambda c, t: (t, c)),
        pl.BlockSpec((T, CB), lambda c, t: (t, c)),
        pl.BlockSpec((conv_w.shape[0], CB), lambda c, t: (0, c)),
        vec(),
        pl.BlockSpec((hpb, hd, hd), lambda c, t: (c, 0, 0)),
        vec(),
        pl.BlockSpec((hpb, hd, hd), lambda c, t: (c, 0, 0)),
        vec(),
        vec(),
    ]
    out_specs = [pl.BlockSpec((T, CB), lambda c, t: (t, c))]
    out_shapes = [jax.ShapeDtypeStruct((n, dl), BF16)]
    args = [u, g, conv_w, conv_b, ga_w, ga_b, gx_w, gx_b, lam]
    plan = None
    if cast_src is not None:
        plan = _cast_plan(cast_src[0], cast_src[3], (dl // CB) * nt, lambda c, t: c * nt + t)
    if plan is not None:
        in_specs += plan[0]
        out_specs += plan[1]
        out_shapes += plan[2]
        args += list(cast_src[:3])
    outs = pl.pallas_call(
        kern,
        out_shape=tuple(out_shapes),
        grid=(dl // CB, nt),
        in_specs=in_specs,
        out_specs=tuple(out_specs),
        scratch_shapes=[
            pltpu.VMEM((T + 8, CB), F32),
            pltpu.VMEM((8, CB), F32),
            pltpu.VMEM((T, CB), F32),
            pltpu.VMEM((T, CB), F32),
        ],
        compiler_params=_cparams(("arbitrary", "arbitrary")),
        name="lru_scan",
    )(*args)
    return outs[0], (tuple(outs[1:]) if plan is not None else None)


def _norm_pack_store(acc, sc_ref, sh_ref, h_ref, nj, tn, d):
    ss = jnp.sum(acc[0] * acc[0], axis=-1, keepdims=True)
    for jj in range(1, nj):
        ss = ss + jnp.sum(acc[jj] * acc[jj], axis=-1, keepdims=True)
    inv = _rms_inv(ss, d)
    half = nj // 2
    for jj in range(half):
        lo_c = slice(jj * tn, (jj + 1) * tn)
        hi_c = slice((jj + half) * tn, (jj + half + 1) * tn)
        lo = (acc[jj] * inv) * (1.0 + sc_ref[:, lo_c]) + sh_ref[:, lo_c]
        hi = (acc[jj + half] * inv) * (1.0 + sc_ref[:, hi_c]) + sh_ref[:, hi_c]
        h_ref[:, lo_c] = _pack_pair(lo, hi)


def _outproj_kernel(z_ref, w_ref, b_ref, x_ref, gate_ref, sc_ref, sh_ref, x1_ref, h_ref, acc,
                    *, NJ, TN, D):
    j = pl.program_id(1)
    y = jnp.dot(z_ref[...], w_ref[...], preferred_element_type=F32) + b_ref[...]
    x1 = x_ref[...] + gate_ref[...] * y
    x1_ref[...] = x1
    acc[j] = x1

    @pl.when(j == NJ - 1)
    def _():
        _norm_pack_store(acc, sc_ref, sh_ref, h_ref, NJ, TN, D)


def _outproj(z, w_out, b_out, x, gate, scale, shift):
    n, dl = z.shape
    d = w_out.shape[1]
    tm = min(512, n)
    tn = min(1024, d // 2)
    nj = d // tn
    kern = functools.partial(_outproj_kernel, NJ=nj, TN=tn, D=d)
    return pl.pallas_call(
        kern,
        out_shape=(jax.ShapeDtypeStruct((n, d), F32), jax.ShapeDtypeStruct((n, d // 2), U32)),
        grid=(n // tm, nj),
        in_specs=[
            pl.BlockSpec((tm, dl), lambda i, j: (i, 0)),
            pl.BlockSpec((dl, tn), lambda i, j: (0, j)),
            pl.BlockSpec((1, tn), lambda i, j: (0, j)),
            pl.BlockSpec((tm, tn), lambda i, j: (i, j)),
            pl.BlockSpec((1, tn), lambda i, j: (0, j)),
            pl.BlockSpec((1, d), lambda i, j: (0, 0)),
            pl.BlockSpec((1, d), lambda i, j: (0, 0)),
        ],
        out_specs=(pl.BlockSpec((tm, tn), lambda i, j: (i, j)),
                   pl.BlockSpec((tm, d // 2), lambda i, j: (i, 0))),
        scratch_shapes=[pltpu.VMEM((nj, tm, tn), F32)],
        compiler_params=_cparams(("arbitrary", "arbitrary")),
        name="lru_outproj",
    )(z, w_out, b_out, x, gate, scale, shift)


def _pool_kernel(x_ref, sc_ref, sh_ref, pw_ref, pb_ref, ps_ref, gate_ref, sc2_ref, sh2_ref,
                 x3_ref, h_ref, ext, s_a, s_b, ybuf, *, T, D, GD):
    t = pl.program_id(0)
    H = POOL_HIST

    @pl.when(t == 0)
    def _():
        ext[0:H, :] = jnp.zeros((H, D), F32)
        s_a[0:8, :] = jnp.zeros((8, GD), F32)
        s_b[0:8, :] = jnp.zeros((8, GD), F32)

    x = x_ref[...]
    inv = _rms_inv(jnp.sum(x * x, axis=-1, keepdims=True), D)
    ext[H:T + H, :] = (x * inv) * (1.0 + sc_ref[...]) + sh_ref[...]

    pos1 = lax.broadcasted_iota(I32, (T, 1), 0) + (t * T + 1)
    for g, w in enumerate(POOL_WINDOWS):
        c = slice(g * GD, (g + 1) * GD)
        if w == 2:
            win = ext[H:T + H, c] + ext[H - 1:T + H - 1, c]
        else:
            s_a[8:T + H, :] = ext[8:T + H, c] + ext[7:T + H - 1, c]
            if w == 4:
                win = s_a[H:T + H, :] + s_a[H - 2:T + H - 2, :]
            else:
                s_b[8:T + H, :] = s_a[8:T + H, :] + s_a[6:T + H - 2, :]
                if w == 8:
                    win = s_b[H:T + H, :] + s_b[H - 4:T + H - 4, :]
                else:
                    s_a[8:T + H, :] = s_b[8:T + H, :] + s_b[4:T + H - 4, :]
                    win = s_a[H:T + H, :] + s_a[H - 8:T + H - 8, :]
        cnt = jnp.minimum(pos1, w).astype(F32)
        pooled = win / cnt - ext[H:T + H, c]
        yg = jnp.dot(pooled.astype(BF16), pw_ref[g], preferred_element_type=F32) + pb_ref[:, c]
        ybuf[:, c] = yg * ps_ref[:, c]
    ext[0:H, :] = ext[T:T + H, :]

    x3 = x + gate_ref[...] * ybuf[...]
    x3_ref[...] = x3
    inv3 = _rms_inv(jnp.sum(x3 * x3, axis=-1, keepdims=True), D)
    h4 = (x3 * inv3) * (1.0 + sc2_ref[...]) + sh2_ref[...]
    h_ref[...] = _pack_pair(h4[:, :D // 2], h4[:, D // 2:])


def _pool_mixer(x, scale, shift, pool_w, pool_b, pool_scale, gate, scale2, shift2):
    n, d = x.shape
    ng, gd, _ = pool_w.shape
    T = min(256, n)
    vec = lambda: pl.BlockSpec((1, d), lambda t: (0, 0))
    kern = functools.partial(_pool_kernel, T=T, D=d, GD=gd)
    return pl.pallas_call(
        kern,
        out_shape=(jax.ShapeDtypeStruct((n, d), F32), jax.ShapeDtypeStruct((n, d // 2), U32)),
        grid=(n // T,),
        in_specs=[
            pl.BlockSpec((T, d), lambda t: (t, 0)),
            vec(), vec(),
            pl.BlockSpec((ng, gd, gd), lambda t: (0, 0, 0)),
            vec(), vec(), vec(), vec(), vec(),
        ],
        out_specs=(pl.BlockSpec((T, d), lambda t: (t, 0)),
                   pl.BlockSpec((T, d // 2), lambda t: (t, 0))),
        scratch_shapes=[
            pltpu.VMEM((T + POOL_HIST, d), F32),
            pltpu.VMEM((T + POOL_HIST, gd), F32),
            pltpu.VMEM((T + POOL_HIST, gd), F32),
            pltpu.VMEM((T, d), F32),
        ],
        compiler_params=_cparams(("arbitrary",)),
        name="pool_mixer",
    )(x, scale, shift, pool_w, pool_b, pool_scale, gate, scale2, shift2)


def _router_kernel(h_ref, wlo_ref, whi_ref, rb_ref, sel_ref, wf_ref, cnt_ref, *, T, E):
    i = pl.program_id(0)
    NG = N_GROUPS
    GS = E // NG
    lo, hi = _unpack_pair(h_ref[...])
    dn = (((1,), (1,)), ((), ()))
    logits = (lax.dot_general(wlo_ref[...], lo.astype(BF16), dn, preferred_element_type=F32)
              + lax.dot_general(whi_ref[...], hi.astype(BF16), dn, preferred_element_type=F32))
    scores = jax.nn.sigmoid(logits).reshape(NG, GS, T)
    biased = scores + rb_ref[...].reshape(NG, GS, 1)

    sub = lax.broadcasted_iota(I32, (NG, GS, T), 1)
    m1 = jnp.max(biased, axis=1, keepdims=True)
    first = jnp.min(jnp.where(biased == m1, sub, GS), axis=1, keepdims=True)
    m2 = jnp.max(jnp.where(sub == first, -jnp.inf, biased), axis=1, keepdims=True)
    gscore = m1 + m2

    gidx = lax.broadcasted_iota(I32, (NG, 1, T), 0)
    beaten = jnp.zeros((NG, 1, T), F32)
    for j in range(NG):
        sj = gscore[j:j + 1]
        beaten = beaten + jnp.where(sj > gscore, 1.0,
                                    jnp.where(sj == gscore, jnp.where(gidx > j, 1.0, 0.0), 0.0))
    masked = jnp.where(beaten < TOPK_GROUPS, biased, -jnp.inf)

    eidx = lax.broadcasted_iota(I32, (NG, GS, T), 0) * GS + sub
    beaten = jnp.zeros((NG, GS, T), F32)
    for g in range(NG):
        mg = masked[g]
        for s in range(GS):
            v = mg[s:s + 1, :][None]
            later = jnp.where(eidx > g * GS + s, 1.0, 0.0)
            beaten = beaten + jnp.where(v > masked, 1.0, jnp.where(v == masked, later, 0.0))
    sel = jnp.where(beaten < TOP_K, 1.0, 0.0)

    picked = sel * scores
    tot = jnp.sum(jnp.sum(picked, axis=1, keepdims=True), axis=0, keepdims=True)
    wf = picked / tot * ROUTED_SCALE

    sel2 = sel.reshape(E, T)
    sel_ref[...] = sel2
    wf_ref[...] = wf.reshape(E, T)

    @pl.when(i == 0)
    def _():
        cnt_ref[...] = jnp.zeros_like(cnt_ref)
    part = sel2[:, 0:128]
    for c in range(1, T // 128):
        part = part + sel2[:, c * 128:(c + 1) * 128]
    cnt_ref[...] += part


def _router(h2, wlo, whi, rbias):
    n, d2 = h2.shape
    e = wlo.shape[0]
    T = min(512, n)
    kern = functools.partial(_router_kernel, T=T, E=e)
    return pl.pallas_call(
        kern,
        out_shape=(jax.ShapeDtypeStruct((e, n), F32), jax.ShapeDtypeStruct((e, n), F32),
                   jax.ShapeDtypeStruct((e, 128), F32)),
        grid=(n // T,),
        in_specs=[
            pl.BlockSpec((T, d2), lambda i: (i, 0)),
            pl.BlockSpec((e, d2), lambda i: (0, 0)),
            pl.BlockSpec((e, d2), lambda i: (0, 0)),
            pl.BlockSpec((e, 1), lambda i: (0, 0)),
        ],
        out_specs=(pl.BlockSpec((e, T), lambda i: (0, i)),
                   pl.BlockSpec((e, T), lambda i: (0, i)),
                   pl.BlockSpec((e, 128), lambda i: (0, 0))),
        compiler_params=_cparams(("arbitrary",)),
        name="moe_router",
    )(h2, wlo, whi, rbias)


def _rank_kernel(sel_ref, wf_ref, ps_ref, dest_ref, wk_ref, carry, *, T, E):
    i = pl.program_id(0)

    @pl.when(i == 0)
    def _():
        carry[...] = jnp.zeros_like(carry)

    sel = sel_ref[...]
    selb = sel.astype(BF16)
    before = jnp.where(lax.broadcasted_iota(I32, (T, T), 0) < lax.broadcasted_iota(I32, (T, T), 1),
                       1.0, 0.0).astype(BF16)
    rank = jnp.dot(selb, before, preferred_element_type=F32)
    destf = ps_ref[...] + carry[...] + rank
    lower = jnp.where(lax.broadcasted_iota(I32, (E, E), 1) < lax.broadcasted_iota(I32, (E, E), 0),
                      1.0, 0.0).astype(BF16)
    slot = jnp.dot(lower, selb, preferred_element_type=F32)
    wf = wf_ref[...]
    for k in range(TOP_K):
        mk = jnp.where(slot == k, sel, 0.0)
        dest_ref[k:k + 1, :] = jnp.sum(mk * destf, axis=0, keepdims=True).astype(I32)
        wk_ref[k:k + 1, :] = jnp.sum(mk * wf, axis=0, keepdims=True)
    carry[...] += jnp.sum(sel, axis=1, keepdims=True)


def _rank(sel, wf, pad_start):
    e, n = sel.shape
    T = min(512, n)
    kern = functools.partial(_rank_kernel, T=T, E=e)
    return pl.pallas_call(
        kern,
        out_shape=(jax.ShapeDtypeStruct((TOP_K, n), I32), jax.ShapeDtypeStruct((TOP_K, n), F32)),
        grid=(n // T,),
        in_specs=[
            pl.BlockSpec((e, T), lambda i: (0, i)),
            pl.BlockSpec((e, T), lambda i: (0, i)),
            pl.BlockSpec((e, 1), lambda i: (0, 0)),
        ],
        out_specs=(pl.BlockSpec((TOP_K, T), lambda i: (0, i)),
                   pl.BlockSpec((TOP_K, T), lambda i: (0, i))),
        scratch_shapes=[pltpu.VMEM((e, 1), F32)],
        compiler_params=_cparams(("arbitrary",)),
        name="moe_rank",
    )(sel, wf, pad_start)


FFN_CHAIN_ROWS = 128


def _ffn_chain(x, wgu_ref, wd_ref):
    d2 = x.shape[1]
    f = wd_ref.shape[0]
    lo, hi = _unpack_pair(x)
    gu = (jnp.dot(lo.astype(BF16), wgu_ref[0:d2, :], preferred_element_type=F32)
          + jnp.dot(hi.astype(BF16), wgu_ref[d2:2 * d2, :], preferred_element_type=F32))
    act = jax.nn.silu(gu[:, :f]) * gu[:, f:]
    y = jnp.dot(act.astype(BF16), wd_ref[...], preferred_element_type=F32)
    return _pack_pair(y[:, :d2], y[:, d2:])


def _dispatch_kernel(zs_ref, zl_ref, dest_ref, h_ref, wsgu_ref, wsd_ref, xs_ref, ysh_ref,
                     zbuf, sem, zsem, *, T, E):
    i = pl.program_id(0)

    def row_copy(t, k):
        return pltpu.make_async_copy(h_ref.at[pl.ds(t, 1), :],
                                     xs_ref.at[pl.ds(dest_ref[k, t], 1), :], sem)

    def for_each_pad_piece(fn):
        def per_expert(e, _):
            start = zs_ref[e]
            ln = zl_ref[e]
            head = jnp.minimum((8 - (start & 7)) & 7, ln)

            def per_row(r, _):
                fn(pltpu.make_async_copy(zbuf.at[pl.ds(0, 1), :],
                                         xs_ref.at[pl.ds(start + r, 1), :], zsem))
                return 0
            lax.fori_loop(0, head, per_row, 0)

            def per_group(q, _):
                r0 = pl.multiple_of(start + head + q * 8, 8)
                fn(pltpu.make_async_copy(zbuf, xs_ref.at[pl.ds(r0, 8), :], zsem))
                return 0
            lax.fori_loop(0, (ln - head) // 8, per_group, 0)
            return 0
        lax.fori_loop(0, E, per_expert, 0)

    @pl.when(i == 0)
    def _():
        zbuf[...] = jnp.zeros_like(zbuf)
        for_each_pad_piece(lambda cp: cp.start())

    def issue(g, _):
        t0 = pl.multiple_of(g * 8, 8)
        for r in range(8):
            for k in range(TOP_K):
                row_copy(t0 + r, k).start()
        return 0
    lax.fori_loop(0, T // 8, issue, 0)

    for s in range(T // FFN_CHAIN_ROWS):
        rs = slice(s * FFN_CHAIN_ROWS, (s + 1) * FFN_CHAIN_ROWS)
        ysh_ref[rs, :] = _ffn_chain(h_ref[rs, :], wsgu_ref, wsd_ref)

    for k in range(TOP_K):
        pltpu.make_async_copy(h_ref, xs_ref.at[pl.ds(0, T), :], sem).wait()

    @pl.when(i == 0)
    def _():
        for_each_pad_piece(lambda cp: cp.wait())


def _dispatch(h2, dest, zero_start, zero_len, n_rows, wsgu, wsd):
    n, d2 = h2.shape
    e = zero_start.shape[0]
    _, d, f2 = wsgu.shape
    T = min(256, n)
    kern = functools.partial(_dispatch_kernel, T=T, E=e)
    grid_spec = pltpu.PrefetchScalarGridSpec(
        num_scalar_prefetch=2,
        grid=(n // T,),
        in_specs=[
            pl.BlockSpec((TOP_K, T), lambda i, zs, zl: (0, i), memory_space=pltpu.SMEM),
            pl.BlockSpec((T, d2), lambda i, zs, zl: (i, 0)),
            pl.BlockSpec((None, d, f2), lambda i, zs, zl: (0, 0, 0)),
            pl.BlockSpec((None, f2 // 2, d), lambda i, zs, zl: (0, 0, 0)),
        ],
        out_specs=(pl.BlockSpec(memory_space=pl.ANY),
                   pl.BlockSpec((T, d2), lambda i, zs, zl: (i, 0))),
        scratch_shapes=[
            pltpu.VMEM((8, d2), U32),
            pltpu.SemaphoreType.DMA(()),
            pltpu.SemaphoreType.DMA(()),
        ],
    )
    return pl.pallas_call(
        kern,
        out_shape=(jax.ShapeDtypeStruct((n_rows, d2), U32), jax.ShapeDtypeStruct((n, d2), U32)),
        grid_spec=grid_spec,
        compiler_params=_cparams(("arbitrary",)),
        name="moe_dispatch",
    )(zero_start, zero_len, dest, h2, wsgu, wsd)


def _ffn_kernel(be_ref, na_ref, x_ref, wgu_ref, wd_ref, *rest):
    y_ref = rest[-3] if len(rest) > 1 else rest[0]
    if len(rest) > 1:
        _cast_body(*rest[:3], *rest[-2:])
    b = pl.program_id(0)

    @pl.when(b < na_ref[0])
    def _():
        for s in range(x_ref.shape[0] // FFN_CHAIN_ROWS):
            rs = slice(s * FFN_CHAIN_ROWS, (s + 1) * FFN_CHAIN_ROWS)
            y_ref[rs, :] = _ffn_chain(x_ref[rs, :], wgu_ref, wd_ref)

    @pl.when(b >= na_ref[0])
    def _():
        y_ref[...] = jnp.zeros_like(y_ref)


def _grouped_ffn(xs, block_expert, n_active, wgu, wd, cast_src=None):
    rows, d2 = xs.shape
    e, d, f2 = wgu.shape
    f = f2 // 2
    nb = rows // EXPERT_BLOCK

    def blk(b, be, na):
        return jnp.minimum(b, na[0] - 1)

    in_specs = [
        pl.BlockSpec((EXPERT_BLOCK, d2), lambda b, be, na: (blk(b, be, na), 0)),
        pl.BlockSpec((None, d, f2), lambda b, be, na: (be[blk(b, be, na)], 0, 0)),
        pl.BlockSpec((None, f, d), lambda b, be, na: (be[blk(b, be, na)], 0, 0)),
    ]
    out_specs = [pl.BlockSpec((EXPERT_BLOCK, d2), lambda b, be, na: (b, 0))]
    out_shapes = [jax.ShapeDtypeStruct((rows, d2), U32)]
    args = [block_expert, n_active, xs, wgu, wd]
    plan = None
    if cast_src is not None:
        plan = _cast_plan(cast_src[0], cast_src[3], nb, lambda b, be, na: b)
    if plan is not None:
        in_specs += plan[0]
        out_specs += plan[1]
        out_shapes += plan[2]
        args += list(cast_src[:3])
    grid_spec = pltpu.PrefetchScalarGridSpec(
        num_scalar_prefetch=2, grid=(nb,), in_specs=in_specs, out_specs=tuple(out_specs))
    outs = pl.pallas_call(
        _ffn_kernel,
        out_shape=tuple(out_shapes),
        grid_spec=grid_spec,
        compiler_params=_cparams(("arbitrary",)),
        name="moe_ffn",
    )(*args)
    return outs[0], (tuple(outs[1:]) if plan is not None else None)


def _combine_kernel(dcur_ref, dnext_ref, wk_ref, ys_ref, ysh_ref, x_ref, gate_ref, gain_ref, o_ref,
                    buf, sems, *, T, D, NT, CW, FINAL):
    i = pl.program_id(0)
    slot = lax.rem(i, 2)
    D2 = D // 2

    def issue(dref, sl, t):
        for k in range(TOP_K):
            pltpu.make_async_copy(ys_ref.at[pl.ds(dref[k, t], 1), :],
                                  buf.at[sl, k, pl.ds(t, 1), :], sems.at[sl]).start()

    def wait_slot(sl):
        for k in range(TOP_K):
            pltpu.make_async_copy(ys_ref.at[pl.ds(0, T), :], buf.at[sl, k], sems.at[sl]).wait()

    @pl.when(i == 0)
    def _():
        def body(t, _):
            issue(dcur_ref, 0, t)
            return 0
        lax.fori_loop(0, T, body, 0)

    wait_slot(slot)

    def row_group(rg, _):
        for tt in range(8):
            issue(dnext_ref, 1 - slot, rg * 8 + tt)
        r = pl.ds(pl.multiple_of(rg * 8, 8), 8)
        w8 = wk_ref[r, :]
        wk = [w8[:, k:k + 1] for k in range(TOP_K)]
        ss = jnp.zeros((8, 1), F32)
        for c in range(D2 // CW):
            c_lo = slice(c * CW, (c + 1) * CW)
            c_hi = slice(D2 + c * CW, D2 + (c + 1) * CW)
            lo, hi = _unpack_pair(ysh_ref[r, c_lo])
            for k in range(TOP_K):
                lo_k, hi_k = _unpack_pair(buf[slot, k, r, c_lo])
                lo = lo + wk[k] * lo_k
                hi = hi + wk[k] * hi_k
            xo_lo = x_ref[r, c_lo] + gate_ref[:, c_lo] * lo
            xo_hi = x_ref[r, c_hi] + gate_ref[:, c_hi] * hi
            if FINAL:
                ss = ss + (jnp.sum(xo_lo * xo_lo, axis=-1, keepdims=True)
                           + jnp.sum(xo_hi * xo_hi, axis=-1, keepdims=True))
            o_ref[r, c_lo] = xo_lo
            o_ref[r, c_hi] = xo_hi
        if FINAL:
            inv = _rms_inv(ss, D)
            for c in range(D // CW):
                cs = slice(c * CW, (c + 1) * CW)
                o_ref[r, cs] = (o_ref[r, cs] * inv) * gain_ref[:, cs]
        return 0
    lax.fori_loop(0, T // 8, row_group, 0, unroll=4)

    @pl.when(i == NT - 1)
    def _():
        wait_slot(1 - slot)


def _combine(dest, wk_t, ys, ysh, x, gate, gain, final):
    n, d = x.shape
    d2 = d // 2
    T = min(128, n)
    nt = n // T
    kern = functools.partial(_combine_kernel, T=T, D=d, NT=nt, CW=min(512, d2), FINAL=final)
    return pl.pallas_call(
        kern,
        out_shape=jax.ShapeDtypeStruct((n, d), F32),
        grid=(nt,),
        in_specs=[
            pl.BlockSpec((TOP_K, T), lambda i: (0, i), memory_space=pltpu.SMEM),
            pl.BlockSpec((TOP_K, T), lambda i: (0, jnp.minimum(i + 1, nt - 1)),
                         memory_space=pltpu.SMEM),
            pl.BlockSpec((T, TOP_K), lambda i: (i, 0)),
            pl.BlockSpec(memory_space=pl.ANY),
            pl.BlockSpec((T, d2), lambda i: (i, 0)),
            pl.BlockSpec((T, d), lambda i: (i, 0)),
            pl.BlockSpec((1, d), lambda i: (0, 0)),
            pl.BlockSpec((1, d), lambda i: (0, 0)),
        ],
        out_specs=pl.BlockSpec((T, d), lambda i: (i, 0)),
        scratch_shapes=[
            pltpu.VMEM((2, TOP_K, T, d2), U32),
            pltpu.SemaphoreType.DMA((2,)),
        ],
        compiler_params=_cparams(("arbitrary",)),
        name="moe_combine",
    )(dest, dest, wk_t, ys, ysh, x, gate, gain)


def _moe(h2, x, gate, gain, final, layer, cast, cast_next, router_w, router_bias,
         w_gate, w_up, w_down, ws_gate, ws_up, ws_down):
    n, d = x.shape
    d2 = d // 2
    e = router_w.shape[1]
    blk = EXPERT_BLOCK

    rw_t = router_w.T.astype(BF16)
    sel, wf, cnt = _router(h2, rw_t[:, :d2], rw_t[:, d2:], router_bias.reshape(e, 1))

    counts = jnp.sum(cnt, axis=1).astype(I32)
    padded = (counts + blk - 1) // blk * blk
    pad_end = jnp.cumsum(padded)
    pad_start = pad_end - padded
    n_blocks = (n * TOP_K + e * (blk - 1) + blk - 1) // blk
    block_start = jnp.arange(n_blocks, dtype=I32) * blk
    block_expert = jnp.minimum(
        jnp.sum((pad_end[None, :] <= block_start[:, None]).astype(I32), axis=1), e - 1)
    n_active = (pad_end[-1:] // blk).astype(I32)

    dest, wk = _rank(sel, wf, pad_start.astype(F32).reshape(e, 1))
    zero_len = (padded - counts).at[e - 1].add(n_blocks * blk - pad_end[-1])
    wsgu, wsd = _cast_experts(ws_gate[:, None], ws_up[:, None], ws_down[:, None], layer)
    xs, ysh = _dispatch(h2, dest, pad_start + counts, zero_len, n_blocks * blk, wsgu, wsd)

    wgu, wd = cast if cast is not None else _cast_experts(w_gate, w_up, w_down, layer)
    ys, next_cast = _grouped_ffn(xs, block_expert, n_active, wgu, wd,
                                 (w_gate, w_up, w_down, layer + 1) if cast_next else None)

    return _combine(dest, wk.T, ys, ysh, x, gate, gain, final), next_cast


def kernel(x, c, mod_w, mod_b, lru_w_in, lru_b_in, lru_conv_w, lru_conv_b, lru_gate_a_w,
           lru_gate_a_b, lru_gate_x_w, lru_gate_x_b, lru_lambda, lru_w_out, lru_b_out, pool_w,
           pool_b, pool_scale, router_w, router_bias, expert_w_gate, expert_w_up, expert_w_down,
           shared_w_gate, shared_w_up, shared_w_down, final_gain):
    bsz, seq, d = x.shape
    depth = mod_w.shape[0]
    assert bsz == 1, "one sequence per call"
    xf = x.reshape(seq, d)
    row = lambda v: v.reshape(1, -1)

    mod = _adaln_all(c, mod_w, mod_b)

    def adaln(i, s):
        m = mod[2 * i + s]
        return m[:, :d], m[:, d:2 * d], m[:, 2 * d:]

    gain = row(final_gain)
    experts = (expert_w_gate, expert_w_up, expert_w_down)
    cast = None
    for i in range(depth):
        j = i // 2
        shift, scale, gate = adaln(i, 0)
        shift2, scale2, gate2 = adaln(i, 1)
        if i % 2 == 0:
            h = _norm_mod(xf, scale, shift)
            g, u = _inproj(h, lru_w_in[j].astype(BF16), row(lru_b_in[j]))
            z, made = _lru_scan(u, g, lru_conv_w[j], row(lru_conv_b[j]),
                                lru_gate_a_w[j].astype(BF16), row(lru_gate_a_b[j]),
                                lru_gate_x_w[j].astype(BF16), row(lru_gate_x_b[j]),
                                row(lru_lambda[j]), (*experts, i) if cast is None else None)
            cast = made if cast is None else cast
            xf, h2 = _outproj(z, lru_w_out[j].astype(BF16), row(lru_b_out[j]), xf, gate,
                              scale2, shift2)
        else:
            xf, h2 = _pool_mixer(xf, scale, shift, pool_w[j].astype(BF16), row(pool_b[j]),
                                 row(pool_scale[j]), gate, scale2, shift2)
        xf, cast = _moe(h2, xf, gate2, gain, i == depth - 1, i, cast, i + 1 < depth,
                        router_w[i], router_bias[i], *experts,
                        shared_w_gate, shared_w_up, shared_w_down)
    return xf.reshape(bsz, seq, d)
```

```python
import functools

import jax
import jax.numpy as jnp
from jax import lax
from jax.experimental import pallas as pl
from jax.experimental.pallas import tpu as pltpu

F32 = jnp.float32
BF16 = jnp.bfloat16
U32 = jnp.uint32
I32 = jnp.int32

EPS = 1e-6
LRU_C = 8.0
N_GROUPS = 8
TOPK_GROUPS = 4
TOP_K = 8
ROUTED_SCALE = 2.5
POOL_WINDOWS = (2, 4, 8, 16)
EXPERT_BLOCK = 512
FFN_CHAIN_ROWS = 128
POOL_HIST = 24
LANES = 128
VMEM_LIMIT = 56 * 1024 * 1024

ADALN_TN = 512
NORM_TM = 512
INPROJ_TM, INPROJ_TN = 1024, 512
SCAN_T, SCAN_CB = 256, 1024
OUTPROJ_TM, OUTPROJ_TN = 512, 1024
POOL_T = 256
ROUTER_T = 512
DISPATCH_T = 256
COMBINE_T, COMBINE_CW = 128, 512


def _cparams(sem, vmem=VMEM_LIMIT):
    return pltpu.CompilerParams(dimension_semantics=sem, vmem_limit_bytes=vmem)


def _pack_pair(lo, hi):
    lo_b = lax.bitcast_convert_type(lo.astype(BF16).astype(F32), U32)
    hi_b = lax.bitcast_convert_type(hi.astype(BF16).astype(F32), U32)
    return hi_b | (lo_b >> 16)


def _unpack_pair(w):
    lo = lax.bitcast_convert_type(w << 16, F32)
    hi = lax.bitcast_convert_type(w & jnp.uint32(0xFFFF0000), F32)
    return lo, hi


def _rms_inv(ss, d):
    return lax.rsqrt(ss / d + EPS)


CAST_SPLITS = (4, 2, 1)


def _cast_body(wg_ref, wu_ref, wd_ref, wgu_ref, wdo_ref):
    f = wg_ref.shape[-1]
    wgu_ref[:, :f] = wg_ref[...].astype(BF16)
    wgu_ref[:, f:] = wu_ref[...].astype(BF16)
    wdo_ref[...] = wd_ref[...].astype(BF16)


def _cast_plan(w_gate, layer, n_steps, step_fn):
    _, e, d, f = w_gate.shape
    q = next((s for s in CAST_SPLITS
              if e * s <= n_steps and f % s == 0 and d % s == 0 and (f // s) % 16 == 0), None)
    if q is None:
        return None
    dq, fq = d // q, f // q

    def piece(*a):
        u = jnp.minimum(step_fn(*a), e * q - 1)
        return u // q, u % q

    def w_in(*a):
        ex, pc = piece(*a)
        return layer, ex, pc, 0

    def w_out(*a):
        ex, pc = piece(*a)
        return ex, pc, 0

    in_specs = [pl.BlockSpec((None, None, dq, f), w_in), pl.BlockSpec((None, None, dq, f), w_in),
                pl.BlockSpec((None, None, fq, d), w_in)]
    out_specs = [pl.BlockSpec((None, dq, 2 * f), w_out), pl.BlockSpec((None, fq, d), w_out)]
    out_shapes = [jax.ShapeDtypeStruct((e, d, 2 * f), BF16), jax.ShapeDtypeStruct((e, f, d), BF16)]
    return in_specs, out_specs, out_shapes


def _cast_experts(w_gate, w_up, w_down, layer):
    e = w_gate.shape[1]
    n_steps = e * CAST_SPLITS[0]
    in_specs, out_specs, out_shapes = _cast_plan(w_gate, layer, n_steps, lambda i: i)
    return pl.pallas_call(
        _cast_body,
        out_shape=tuple(out_shapes),
        grid=(n_steps,),
        in_specs=in_specs,
        out_specs=tuple(out_specs),
        compiler_params=_cparams(("arbitrary",)),
        name="expert_cast",
    )(w_gate, w_up, w_down)


def _mod_kernel(c_ref, w_ref, b_ref, o_ref):
    c = c_ref[...]
    s = c * jax.nn.sigmoid(c)
    s8 = jnp.broadcast_to(s, (8, s.shape[1])).astype(BF16)
    acc = jnp.dot(s8, w_ref[...].astype(BF16), preferred_element_type=F32)
    o_ref[...] = acc[0:1, :] + b_ref[...]


def _adaln_all(c, mod_w, mod_b):
    depth, two, d, d3 = mod_w.shape
    na = depth * two
    tn = ADALN_TN
    w = mod_w.reshape(na, d, d3)
    b = mod_b.reshape(na, 1, d3)
    return pl.pallas_call(
        _mod_kernel,
        out_shape=jax.ShapeDtypeStruct((na, 1, d3), F32),
        grid=(na, d3 // tn),
        in_specs=[
            pl.BlockSpec((1, d), lambda a, j: (0, 0)),
            pl.BlockSpec((None, d, tn), lambda a, j: (a, 0, j)),
            pl.BlockSpec((None, 1, tn), lambda a, j: (a, 0, j)),
        ],
        out_specs=pl.BlockSpec((None, 1, tn), lambda a, j: (a, 0, j)),
        compiler_params=_cparams(("arbitrary", "arbitrary")),
        name="adaln_mod",
    )(c, w, b)


def _norm_mod_kernel(x_ref, sc_ref, sh_ref, o_ref):
    x = x_ref[...]
    inv = _rms_inv(jnp.sum(x * x, axis=-1, keepdims=True), x.shape[-1])
    o_ref[...] = ((x * inv) * (1.0 + sc_ref[...]) + sh_ref[...]).astype(o_ref.dtype)


def _norm_mod(x, scale, shift):
    n, d = x.shape
    tm = min(NORM_TM, n)
    return pl.pallas_call(
        _norm_mod_kernel,
        out_shape=jax.ShapeDtypeStruct((n, d), BF16),
        grid=(n // tm,),
        in_specs=[
            pl.BlockSpec((tm, d), lambda i: (i, 0)),
            pl.BlockSpec((1, d), lambda i: (0, 0)),
            pl.BlockSpec((1, d), lambda i: (0, 0)),
        ],
        out_specs=pl.BlockSpec((tm, d), lambda i: (i, 0)),
        compiler_params=_cparams(("arbitrary",)),
        name="norm_mod",
    )(x, scale, shift)


def _inproj_kernel(h_ref, wg_ref, wu_ref, bg_ref, bu_ref, g_ref, u_ref):
    h = h_ref[...]
    gate = jnp.dot(h, wg_ref[...], preferred_element_type=F32) + bg_ref[...]
    g_ref[...] = jax.nn.gelu(gate, approximate=True).astype(g_ref.dtype)
    u_ref[...] = jnp.dot(h, wu_ref[...], preferred_element_type=F32) + bu_ref[...]


def _inproj(h, w_in, b_in):
    n, d = h.shape
    dl = w_in.shape[1] // 2
    tm = min(INPROJ_TM, n)
    tn = min(INPROJ_TN, dl)
    nj = dl // tn
    return pl.pallas_call(
        _inproj_kernel,
        out_shape=(jax.ShapeDtypeStruct((n, dl), BF16), jax.ShapeDtypeStruct((n, dl), F32)),
        grid=(n // tm, nj),
        in_specs=[
            pl.BlockSpec((tm, d), lambda i, j: (i, 0)),
            pl.BlockSpec((d, tn), lambda i, j: (0, j)),
            pl.BlockSpec((d, tn), lambda i, j: (0, j + nj)),
            pl.BlockSpec((1, tn), lambda i, j: (0, j)),
            pl.BlockSpec((1, tn), lambda i, j: (0, j + nj)),
        ],
        out_specs=(pl.BlockSpec((tm, tn), lambda i, j: (i, j)),
                   pl.BlockSpec((tm, tn), lambda i, j: (i, j))),
        compiler_params=_cparams(("arbitrary", "arbitrary")),
        name="lru_inproj",
    )(h, w_in, w_in, b_in, b_in)


def _lru_kernel(u_ref, g_ref, cw_ref, cb_ref, gaw_ref, gab_ref, gxw_ref, gxb_ref, lam_ref, *rest,
                T, CB, HD):
    uext, hcar, a_s, b_s = rest[-4:]
    z_ref = rest[-5] if len(rest) == 5 else rest[3]
    if len(rest) > 5:
        _cast_body(*rest[:3], *rest[4:6])
    t = pl.program_id(1)

    @pl.when(t == 0)
    def _():
        uext[0:8, :] = jnp.zeros((8, CB), F32)
        hcar[...] = jnp.zeros((8, CB), F32)

    u = u_ref[...]
    tail = uext[...]
    cw = cw_ref[...]
    row8 = lax.broadcasted_iota(I32, (8, CB), 0)
    xc = cb_ref[...]
    for d in (3, 2, 1):
        sh = pltpu.roll(u, d, axis=0)
        top = jnp.where(row8 < d, pltpu.roll(tail, d, axis=0), sh[0:8, :])
        xc = xc + cw[3 - d:4 - d, :] * jnp.concatenate([top, sh[8:, :]], axis=0)
    xc = xc + cw[3:4, :] * u
    uext[...] = u[T - 8:T, :]

    xcb = xc.astype(BF16)
    rs, xs = [], []
    for h in range(CB // HD):
        xh = xcb[:, h * HD:(h + 1) * HD]
        rs.append(jnp.dot(xh, gaw_ref[h], preferred_element_type=F32))
        xs.append(jnp.dot(xh, gxw_ref[h], preferred_element_type=F32))
    r = jax.nn.sigmoid(jnp.concatenate(rs, axis=1) + gab_ref[...])
    i = jax.nn.sigmoid(jnp.concatenate(xs, axis=1) + gxb_ref[...])
    lam = lam_ref[...]
    log_sig = jnp.minimum(lam, 0.0) - jnp.log(1.0 + jnp.exp(-jnp.abs(lam)))
    a = jnp.exp((LRU_C * r) * log_sig)
    b = jnp.sqrt(1.0 - a * a) * (i * xc)

    G = T // 8
    A = a.reshape(G, 8, CB)
    B = b.reshape(G, 8, CB)
    row = lax.broadcasted_iota(I32, (G, 8, CB), 1)
    for d in (1, 2, 4):
        keep = row >= d
        B = jnp.where(keep, A * pltpu.roll(B, d, axis=1) + B, B)
        A = jnp.where(keep, A * pltpu.roll(A, d, axis=1), A)
    a_s[...] = A.reshape(T, CB)
    b_s[...] = B.reshape(T, CB)

    def carry_step(gi, hprev):
        r0 = pl.multiple_of(gi * 8, 8)
        hrows = b_s[pl.ds(r0, 8), :] + a_s[pl.ds(r0, 8), :] * hprev
        b_s[pl.ds(r0, 8), :] = hrows
        return jnp.broadcast_to(hrows[7:8, :], (8, CB))

    hcar[...] = lax.fori_loop(0, G, carry_step, hcar[...])
    z_ref[...] = (b_s[...] * g_ref[...].astype(F32)).astype(z_ref.dtype)


def _lru_scan(u, g, conv_w, conv_b, ga_w, ga_b, gx_w, gx_b, lam, cast_src=None):
    n, dl = u.shape
    nh, hd, _ = ga_w.shape
    T = min(SCAN_T, n)
    CB = min(SCAN_CB, dl)
    hpb = CB // hd
    nt = n // T
    vec = lambda: pl.BlockSpec((1, CB), lambda c, t: (0, c))
    kern = functools.partial(_lru_kernel, T=T, CB=CB, HD=hd)
    in_specs = [
        pl.BlockSpec((T, CB), lambda c, t: (t, c)),
        pl.BlockSpec((T, CB), lambda c, t: (t, c)),
        pl.BlockSpec((conv_w.shape[0], CB), lambda c, t: (0, c)),
        vec(),
        pl.BlockSpec((hpb, hd, hd), lambda c, t: (c, 0, 0)),
        vec(),
        pl.BlockSpec((hpb, hd, hd), lambda c, t: (c, 0, 0)),
        vec(),
        vec(),
    ]
    out_specs = [pl.BlockSpec((T, CB), lambda c, t: (t, c))]
    out_shapes = [jax.ShapeDtypeStruct((n, dl), BF16)]
    args = [u, g, conv_w, conv_b, ga_w, ga_b, gx_w, gx_b, lam]
    plan = None
    if cast_src is not None:
        plan = _cast_plan(cast_src[0], cast_src[3], (dl // CB) * nt, lambda c, t: c * nt + t)
    if plan is not None:
        in_specs += plan[0]
        out_specs += plan[1]
        out_shapes += plan[2]
        args += list(cast_src[:3])
    outs = pl.pallas_call(
        kern,
        out_shape=tuple(out_shapes),
        grid=(dl // CB, nt),
        in_specs=in_specs,
        out_specs=tuple(out_specs),
        scratch_shapes=[
            pltpu.VMEM((8, CB), F32),
            pltpu.VMEM((8, CB), F32),
            pltpu.VMEM((T, CB), F32),
            pltpu.VMEM((T, CB), F32),
        ],
        compiler_params=_cparams(("arbitrary", "arbitrary")),
        name="lru_scan",
    )(*args)
    return outs[0], (tuple(outs[1:]) if plan is not None else None)


def _norm_pack_store(acc, sc_ref, sh_ref, h_ref, nj, tn, d):
    ss = jnp.sum(acc[0] * acc[0], axis=-1, keepdims=True)
    for jj in range(1, nj):
        ss = ss + jnp.sum(acc[jj] * acc[jj], axis=-1, keepdims=True)
    inv = _rms_inv(ss, d)
    half = nj // 2
    for jj in range(half):
        lo_c = slice(jj * tn, (jj + 1) * tn)
        hi_c = slice((jj + half) * tn, (jj + half + 1) * tn)
        lo = (acc[jj] * inv) * (1.0 + sc_ref[:, lo_c]) + sh_ref[:, lo_c]
        hi = (acc[jj + half] * inv) * (1.0 + sc_ref[:, hi_c]) + sh_ref[:, hi_c]
        h_ref[:, lo_c] = _pack_pair(lo, hi)


def _outproj_kernel(z_ref, w_ref, b_ref, x_ref, gate_ref, sc_ref, sh_ref, x1_ref, h_ref, acc,
                    *, NJ, TN, D):
    j = pl.program_id(1)
    y = jnp.dot(z_ref[...], w_ref[...], preferred_element_type=F32) + b_ref[...]
    x1 = x_ref[...] + gate_ref[...] * y
    x1_ref[...] = x1
    acc[j] = x1

    @pl.when(j == NJ - 1)
    def _():
        _norm_pack_store(acc, sc_ref, sh_ref, h_ref, NJ, TN, D)


def _outproj(z, w_out, b_out, x, gate, scale, shift):
    n, dl = z.shape
    d = w_out.shape[1]
    tm = min(OUTPROJ_TM, n)
    tn = min(OUTPROJ_TN, d // 2)
    nj = d // tn
    kern = functools.partial(_outproj_kernel, NJ=nj, TN=tn, D=d)
    return pl.pallas_call(
        kern,
        out_shape=(jax.ShapeDtypeStruct((n, d), F32), jax.ShapeDtypeStruct((n, d // 2), U32)),
        grid=(n // tm, nj),
        in_specs=[
            pl.BlockSpec((tm, dl), lambda i, j: (i, 0)),
            pl.BlockSpec((dl, tn), lambda i, j: (0, j)),
            pl.BlockSpec((1, tn), lambda i, j: (0, j)),
            pl.BlockSpec((tm, tn), lambda i, j: (i, j)),
            pl.BlockSpec((1, tn), lambda i, j: (0, j)),
            pl.BlockSpec((1, d), lambda i, j: (0, 0)),
            pl.BlockSpec((1, d), lambda i, j: (0, 0)),
        ],
        out_specs=(pl.BlockSpec((tm, tn), lambda i, j: (i, j)),
                   pl.BlockSpec((tm, d // 2), lambda i, j: (i, 0))),
        scratch_shapes=[pltpu.VMEM((nj, tm, tn), F32)],
        compiler_params=_cparams(("arbitrary", "arbitrary")),
        name="lru_outproj",
    )(z, w_out, b_out, x, gate, scale, shift)


def _pool_kernel(x_ref, sc_ref, sh_ref, pw_ref, pb_ref, ps_ref, gate_ref, sc2_ref, sh2_ref,
                 x3_ref, h_ref, ext, s_a, s_b, ybuf, *, T, D, GD):
    t = pl.program_id(0)
    H = POOL_HIST

    @pl.when(t == 0)
    def _():
        ext[0:H, :] = jnp.zeros((H, D), F32)
        s_a[0:8, :] = jnp.zeros((8, GD), F32)
        s_b[0:8, :] = jnp.zeros((8, GD), F32)

    x = x_ref[...]
    inv = _rms_inv(jnp.sum(x * x, axis=-1, keepdims=True), D)
    ext[H:T + H, :] = (x * inv) * (1.0 + sc_ref[...]) + sh_ref[...]

    pos1 = lax.broadcasted_iota(I32, (T, 1), 0) + (t * T + 1)
    for g, w in enumerate(POOL_WINDOWS):
        c = slice(g * GD, (g + 1) * GD)
        if w == 2:
            win = ext[H:T + H, c] + ext[H - 1:T + H - 1, c]
        else:
            s_a[8:T + H, :] = ext[8:T + H, c] + ext[7:T + H - 1, c]
            if w == 4:
                win = s_a[H:T + H, :] + s_a[H - 2:T + H - 2, :]
            else:
                s_b[8:T + H, :] = s_a[8:T + H, :] + s_a[6:T + H - 2, :]
                if w == 8:
                    win = s_b[H:T + H, :] + s_b[H - 4:T + H - 4, :]
                else:
                    s_a[8:T + H, :] = s_b[8:T + H, :] + s_b[4:T + H - 4, :]
                    win = s_a[H:T + H, :] + s_a[H - 8:T + H - 8, :]
        cnt = jnp.minimum(pos1, w).astype(F32)
        pooled = win / cnt - ext[H:T + H, c]
        yg = jnp.dot(pooled.astype(BF16), pw_ref[g], preferred_element_type=F32) + pb_ref[:, c]
        ybuf[:, c] = yg * ps_ref[:, c]
    ext[0:H, :] = ext[T:T + H, :]

    x3 = x + gate_ref[...] * ybuf[...]
    x3_ref[...] = x3
    inv3 = _rms_inv(jnp.sum(x3 * x3, axis=-1, keepdims=True), D)
    h4 = (x3 * inv3) * (1.0 + sc2_ref[...]) + sh2_ref[...]
    h_ref[...] = _pack_pair(h4[:, :D // 2], h4[:, D // 2:])


def _pool_mixer(x, scale, shift, pool_w, pool_b, pool_scale, gate, scale2, shift2):
    n, d = x.shape
    ng, gd, _ = pool_w.shape
    T = min(POOL_T, n)
    vec = lambda: pl.BlockSpec((1, d), lambda t: (0, 0))
    kern = functools.partial(_pool_kernel, T=T, D=d, GD=gd)
    return pl.pallas_call(
        kern,
        out_shape=(jax.ShapeDtypeStruct((n, d), F32), jax.ShapeDtypeStruct((n, d // 2), U32)),
        grid=(n // T,),
        in_specs=[
            pl.BlockSpec((T, d), lambda t: (t, 0)),
            vec(), vec(),
            pl.BlockSpec((ng, gd, gd), lambda t: (0, 0, 0)),
            vec(), vec(), vec(), vec(), vec(),
        ],
        out_specs=(pl.BlockSpec((T, d), lambda t: (t, 0)),
                   pl.BlockSpec((T, d // 2), lambda t: (t, 0))),
        scratch_shapes=[
            pltpu.VMEM((T + POOL_HIST, d), F32),
            pltpu.VMEM((T + POOL_HIST, gd), F32),
            pltpu.VMEM((T + POOL_HIST, gd), F32),
            pltpu.VMEM((T, d), F32),
        ],
        compiler_params=_cparams(("arbitrary",)),
        name="pool_mixer",
    )(x, scale, shift, pool_w, pool_b, pool_scale, gate, scale2, shift2)


def _router_kernel(h_ref, wlo_ref, whi_ref, rb_ref, sel_ref, wf_ref, cnt_ref, *, T, E):
    i = pl.program_id(0)
    NG = N_GROUPS
    GS = E // NG
    lo, hi = _unpack_pair(h_ref[...])
    dn = (((1,), (1,)), ((), ()))
    logits = (lax.dot_general(wlo_ref[...], lo.astype(BF16), dn, preferred_element_type=F32)
              + lax.dot_general(whi_ref[...], hi.astype(BF16), dn, preferred_element_type=F32))
    scores = jax.nn.sigmoid(logits).reshape(NG, GS, T)
    biased = scores + rb_ref[...].reshape(NG, GS, 1)

    sub = lax.broadcasted_iota(I32, (NG, GS, T), 1)
    m1 = jnp.max(biased, axis=1, keepdims=True)
    first = jnp.min(jnp.where(biased == m1, sub, GS), axis=1, keepdims=True)
    m2 = jnp.max(jnp.where(sub == first, -jnp.inf, biased), axis=1, keepdims=True)
    gscore = m1 + m2

    gidx = lax.broadcasted_iota(I32, (NG, 1, T), 0)
    beaten = jnp.zeros((NG, 1, T), F32)
    for j in range(NG):
        sj = gscore[j:j + 1]
        beaten = beaten + jnp.where(sj > gscore, 1.0,
                                    jnp.where(sj == gscore, jnp.where(gidx > j, 1.0, 0.0), 0.0))
    masked = jnp.where(beaten < TOPK_GROUPS, biased, -jnp.inf)

    eidx = lax.broadcasted_iota(I32, (NG, GS, T), 0) * GS + sub
    beaten = jnp.zeros((NG, GS, T), F32)
    for g in range(NG):
        mg = masked[g]
        for s in range(GS):
            v = mg[s:s + 1, :][None]
            later = jnp.where(eidx > g * GS + s, 1.0, 0.0)
            beaten = beaten + jnp.where(v > masked, 1.0, jnp.where(v == masked, later, 0.0))
    sel = jnp.where(beaten < TOP_K, 1.0, 0.0)

    picked = sel * scores
    tot = jnp.sum(jnp.sum(picked, axis=1, keepdims=True), axis=0, keepdims=True)
    wf = picked / tot * ROUTED_SCALE

    sel2 = sel.reshape(E, T)
    sel_ref[...] = sel2
    wf_ref[...] = wf.reshape(E, T)

    @pl.when(i == 0)
    def _():
        cnt_ref[...] = jnp.zeros_like(cnt_ref)
    part = sel2[:, 0:LANES]
    for c in range(1, T // LANES):
        part = part + sel2[:, c * LANES:(c + 1) * LANES]
    cnt_ref[...] += part


def _router(h2, wlo, whi, rbias):
    n, d2 = h2.shape
    e = wlo.shape[0]
    T = min(ROUTER_T, n)
    kern = functools.partial(_router_kernel, T=T, E=e)
    return pl.pallas_call(
        kern,
        out_shape=(jax.ShapeDtypeStruct((e, n), F32), jax.ShapeDtypeStruct((e, n), F32),
                   jax.ShapeDtypeStruct((e, LANES), F32)),
        grid=(n // T,),
        in_specs=[
            pl.BlockSpec((T, d2), lambda i: (i, 0)),
            pl.BlockSpec((e, d2), lambda i: (0, 0)),
            pl.BlockSpec((e, d2), lambda i: (0, 0)),
            pl.BlockSpec((e, 1), lambda i: (0, 0)),
        ],
        out_specs=(pl.BlockSpec((e, T), lambda i: (0, i)),
                   pl.BlockSpec((e, T), lambda i: (0, i)),
                   pl.BlockSpec((e, LANES), lambda i: (0, 0))),
        compiler_params=_cparams(("arbitrary",)),
        name="moe_router",
    )(h2, wlo, whi, rbias)


def _rank_kernel(sel_ref, wf_ref, ps_ref, dest_ref, wk_ref, carry, *, T, E):
    i = pl.program_id(0)

    @pl.when(i == 0)
    def _():
        carry[...] = jnp.zeros_like(carry)

    sel = sel_ref[...]
    selb = sel.astype(BF16)
    before = jnp.where(lax.broadcasted_iota(I32, (T, T), 0) < lax.broadcasted_iota(I32, (T, T), 1),
                       1.0, 0.0).astype(BF16)
    rank = jnp.dot(selb, before, preferred_element_type=F32)
    destf = ps_ref[...] + carry[...] + rank
    lower = jnp.where(lax.broadcasted_iota(I32, (E, E), 1) < lax.broadcasted_iota(I32, (E, E), 0),
                      1.0, 0.0).astype(BF16)
    slot = jnp.dot(lower, selb, preferred_element_type=F32)
    wf = wf_ref[...]
    for k in range(TOP_K):
        mk = jnp.where(slot == k, sel, 0.0)
        dest_ref[k:k + 1, :] = jnp.sum(mk * destf, axis=0, keepdims=True).astype(I32)
        wk_ref[k:k + 1, :] = jnp.sum(mk * wf, axis=0, keepdims=True)
    carry[...] += jnp.sum(sel, axis=1, keepdims=True)


def _rank(sel, wf, pad_start):
    e, n = sel.shape
    T = min(ROUTER_T, n)
    kern = functools.partial(_rank_kernel, T=T, E=e)
    return pl.pallas_call(
        kern,
        out_shape=(jax.ShapeDtypeStruct((TOP_K, n), I32), jax.ShapeDtypeStruct((TOP_K, n), F32)),
        grid=(n // T,),
        in_specs=[
            pl.BlockSpec((e, T), lambda i: (0, i)),
            pl.BlockSpec((e, T), lambda i: (0, i)),
            pl.BlockSpec((e, 1), lambda i: (0, 0)),
        ],
        out_specs=(pl.BlockSpec((TOP_K, T), lambda i: (0, i)),
                   pl.BlockSpec((TOP_K, T), lambda i: (0, i))),
        scratch_shapes=[pltpu.VMEM((e, 1), F32)],
        compiler_params=_cparams(("arbitrary",)),
        name="moe_rank",
    )(sel, wf, pad_start)


def _ffn_chain(x, wgu_ref, wd_ref):
    d2 = x.shape[1]
    f = wd_ref.shape[0]
    lo, hi = _unpack_pair(x)
    gu = (jnp.dot(lo.astype(BF16), wgu_ref[0:d2, :], preferred_element_type=F32)
          + jnp.dot(hi.astype(BF16), wgu_ref[d2:2 * d2, :], preferred_element_type=F32))
    act = jax.nn.silu(gu[:, :f]) * gu[:, f:]
    y = jnp.dot(act.astype(BF16), wd_ref[...], preferred_element_type=F32)
    return _pack_pair(y[:, :d2], y[:, d2:])


def _dispatch_kernel(zs_ref, zl_ref, dest_ref, h_ref, wsgu_ref, wsd_ref, xs_ref, ysh_ref,
                     zbuf, sem, zsem, *, T, E):
    i = pl.program_id(0)

    def row_copy(t, k):
        return pltpu.make_async_copy(h_ref.at[pl.ds(t, 1), :],
                                     xs_ref.at[pl.ds(dest_ref[k, t], 1), :], sem)

    def for_each_pad_piece(fn):
        def per_expert(e, _):
            start = zs_ref[e]
            ln = zl_ref[e]
            head = jnp.minimum((8 - (start & 7)) & 7, ln)

            def per_row(r, _):
                fn(pltpu.make_async_copy(zbuf.at[pl.ds(0, 1), :],
                                         xs_ref.at[pl.ds(start + r, 1), :], zsem))
                return 0
            lax.fori_loop(0, head, per_row, 0)

            def per_group(q, _):
                r0 = pl.multiple_of(start + head + q * 8, 8)
                fn(pltpu.make_async_copy(zbuf, xs_ref.at[pl.ds(r0, 8), :], zsem))
                return 0
            lax.fori_loop(0, (ln - head) // 8, per_group, 0)
            return 0
        lax.fori_loop(0, E, per_expert, 0)

    @pl.when(i == 0)
    def _():
        zbuf[...] = jnp.zeros_like(zbuf)
        for_each_pad_piece(lambda cp: cp.start())

    def issue(g, _):
        t0 = pl.multiple_of(g * 8, 8)
        for r in range(8):
            for k in range(TOP_K):
                row_copy(t0 + r, k).start()
        return 0
    lax.fori_loop(0, T // 8, issue, 0)

    for s in range(T // FFN_CHAIN_ROWS):
        rs = slice(s * FFN_CHAIN_ROWS, (s + 1) * FFN_CHAIN_ROWS)
        ysh_ref[rs, :] = _ffn_chain(h_ref[rs, :], wsgu_ref, wsd_ref)

    for k in range(TOP_K):
        pltpu.make_async_copy(h_ref, xs_ref.at[pl.ds(0, T), :], sem).wait()

    @pl.when(i == 0)
    def _():
        for_each_pad_piece(lambda cp: cp.wait())


def _dispatch(h2, dest, zero_start, zero_len, n_rows, wsgu, wsd):
    n, d2 = h2.shape
    e = zero_start.shape[0]
    _, d, f2 = wsgu.shape
    T = min(DISPATCH_T, n)
    kern = functools.partial(_dispatch_kernel, T=T, E=e)
    grid_spec = pltpu.PrefetchScalarGridSpec(
        num_scalar_prefetch=2,
        grid=(n // T,),
        in_specs=[
            pl.BlockSpec((TOP_K, T), lambda i, zs, zl: (0, i), memory_space=pltpu.SMEM),
            pl.BlockSpec((T, d2), lambda i, zs, zl: (i, 0)),
            pl.BlockSpec((None, d, f2), lambda i, zs, zl: (0, 0, 0)),
            pl.BlockSpec((None, f2 // 2, d), lambda i, zs, zl: (0, 0, 0)),
        ],
        out_specs=(pl.BlockSpec(memory_space=pl.ANY),
                   pl.BlockSpec((T, d2), lambda i, zs, zl: (i, 0))),
        scratch_shapes=[
            pltpu.VMEM((8, d2), U32),
            pltpu.SemaphoreType.DMA(()),
            pltpu.SemaphoreType.DMA(()),
        ],
    )
    return pl.pallas_call(
        kern,
        out_shape=(jax.ShapeDtypeStruct((n_rows, d2), U32), jax.ShapeDtypeStruct((n, d2), U32)),
        grid_spec=grid_spec,
        compiler_params=_cparams(("arbitrary",)),
        name="moe_dispatch",
    )(zero_start, zero_len, dest, h2, wsgu, wsd)


def _ffn_kernel(be_ref, na_ref, x_ref, wgu_ref, wd_ref, *rest):
    y_ref = rest[-3] if len(rest) > 1 else rest[0]
    if len(rest) > 1:
        _cast_body(*rest[:3], *rest[-2:])
    b = pl.program_id(0)

    @pl.when(b < na_ref[0])
    def _():
        for s in range(x_ref.shape[0] // FFN_CHAIN_ROWS):
            rs = slice(s * FFN_CHAIN_ROWS, (s + 1) * FFN_CHAIN_ROWS)
            y_ref[rs, :] = _ffn_chain(x_ref[rs, :], wgu_ref, wd_ref)

    @pl.when(b >= na_ref[0])
    def _():
        y_ref[...] = jnp.zeros_like(y_ref)


def _grouped_ffn(xs, block_expert, n_active, wgu, wd, cast_src=None):
    rows, d2 = xs.shape
    e, d, f2 = wgu.shape
    f = f2 // 2
    nb = rows // EXPERT_BLOCK

    def blk(b, be, na):
        return jnp.minimum(b, na[0] - 1)

    in_specs = [
        pl.BlockSpec((EXPERT_BLOCK, d2), lambda b, be, na: (blk(b, be, na), 0)),
        pl.BlockSpec((None, d, f2), lambda b, be, na: (be[blk(b, be, na)], 0, 0)),
        pl.BlockSpec((None, f, d), lambda b, be, na: (be[blk(b, be, na)], 0, 0)),
    ]
    out_specs = [pl.BlockSpec((EXPERT_BLOCK, d2), lambda b, be, na: (b, 0))]
    out_shapes = [jax.ShapeDtypeStruct((rows, d2), U32)]
    args = [block_expert, n_active, xs, wgu, wd]
    plan = None
    if cast_src is not None:
        plan = _cast_plan(cast_src[0], cast_src[3], nb, lambda b, be, na: b)
    if plan is not None:
        in_specs += plan[0]
        out_specs += plan[1]
        out_shapes += plan[2]
        args += list(cast_src[:3])
    grid_spec = pltpu.PrefetchScalarGridSpec(
        num_scalar_prefetch=2, grid=(nb,), in_specs=in_specs, out_specs=tuple(out_specs))
    outs = pl.pallas_call(
        _ffn_kernel,
        out_shape=tuple(out_shapes),
        grid_spec=grid_spec,
        compiler_params=_cparams(("arbitrary",)),
        name="moe_ffn",
    )(*args)
    return outs[0], (tuple(outs[1:]) if plan is not None else None)


def _combine_kernel(dcur_ref, dnext_ref, wk_ref, ys_ref, ysh_ref, x_ref, gate_ref, gain_ref, o_ref,
                    buf, sems, *, T, D, NT, CW, FINAL):
    i = pl.program_id(0)
    slot = lax.rem(i, 2)
    D2 = D // 2

    def issue(dref, sl, t):
        for k in range(TOP_K):
            pltpu.make_async_copy(ys_ref.at[pl.ds(dref[k, t], 1), :],
                                  buf.at[sl, k, pl.ds(t, 1), :], sems.at[sl]).start()

    def wait_slot(sl):
        for k in range(TOP_K):
            pltpu.make_async_copy(ys_ref.at[pl.ds(0, T), :], buf.at[sl, k], sems.at[sl]).wait()

    @pl.when(i == 0)
    def _():
        def body(t, _):
            issue(dcur_ref, 0, t)
            return 0
        lax.fori_loop(0, T, body, 0)

    wait_slot(slot)

    def row_group(rg, _):
        for tt in range(8):
            issue(dnext_ref, 1 - slot, rg * 8 + tt)
        r = pl.ds(pl.multiple_of(rg * 8, 8), 8)
        w8 = wk_ref[r, :]
        wk = [w8[:, k:k + 1] for k in range(TOP_K)]
        ss = jnp.zeros((8, 1), F32)
        for c in range(D2 // CW):
            c_lo = slice(c * CW, (c + 1) * CW)
            c_hi = slice(D2 + c * CW, D2 + (c + 1) * CW)
            lo, hi = _unpack_pair(ysh_ref[r, c_lo])
            for k in range(TOP_K):
                lo_k, hi_k = _unpack_pair(buf[slot, k, r, c_lo])
                lo = lo + wk[k] * lo_k
                hi = hi + wk[k] * hi_k
            xo_lo = x_ref[r, c_lo] + gate_ref[:, c_lo] * lo
            xo_hi = x_ref[r, c_hi] + gate_ref[:, c_hi] * hi
            if FINAL:
                ss = ss + (jnp.sum(xo_lo * xo_lo, axis=-1, keepdims=True)
                           + jnp.sum(xo_hi * xo_hi, axis=-1, keepdims=True))
            o_ref[r, c_lo] = xo_lo
            o_ref[r, c_hi] = xo_hi
        if FINAL:
            inv = _rms_inv(ss, D)
            for c in range(D // CW):
                cs = slice(c * CW, (c + 1) * CW)
                o_ref[r, cs] = (o_ref[r, cs] * inv) * gain_ref[:, cs]
        return 0
    lax.fori_loop(0, T // 8, row_group, 0, unroll=4)

    @pl.when(i == NT - 1)
    def _():
        wait_slot(1 - slot)


def _combine(dest, wk_t, ys, ysh, x, gate, gain, final):
    n, d = x.shape
    d2 = d // 2
    T = min(COMBINE_T, n)
    nt = n // T
    kern = functools.partial(_combine_kernel, T=T, D=d, NT=nt, CW=min(COMBINE_CW, d2), FINAL=final)
    return pl.pallas_call(
        kern,
        out_shape=jax.ShapeDtypeStruct((n, d), F32),
        grid=(nt,),
        in_specs=[
            pl.BlockSpec((TOP_K, T), lambda i: (0, i), memory_space=pltpu.SMEM),
            pl.BlockSpec((TOP_K, T), lambda i: (0, jnp.minimum(i + 1, nt - 1)),
                         memory_space=pltpu.SMEM),
            pl.BlockSpec((T, TOP_K), lambda i: (i, 0)),
            pl.BlockSpec(memory_space=pl.ANY),
            pl.BlockSpec((T, d2), lambda i: (i, 0)),
            pl.BlockSpec((T, d), lambda i: (i, 0)),
            pl.BlockSpec((1, d), lambda i: (0, 0)),
            pl.BlockSpec((1, d), lambda i: (0, 0)),
        ],
        out_specs=pl.BlockSpec((T, d), lambda i: (i, 0)),
        scratch_shapes=[
            pltpu.VMEM((2, TOP_K, T, d2), U32),
            pltpu.SemaphoreType.DMA((2,)),
        ],
        compiler_params=_cparams(("arbitrary",)),
        name="moe_combine",
    )(dest, dest, wk_t, ys, ysh, x, gate, gain)


def _moe(h2, x, gate, gain, final, layer, cast, cast_next, router_w, router_bias,
         w_gate, w_up, w_down, ws_gate, ws_up, ws_down):
    n, d = x.shape
    d2 = d // 2
    e = router_w.shape[1]
    blk = EXPERT_BLOCK

    rw_t = router_w.T.astype(BF16)
    sel, wf, cnt = _router(h2, rw_t[:, :d2], rw_t[:, d2:], router_bias.reshape(e, 1))

    counts = jnp.sum(cnt, axis=1).astype(I32)
    padded = (counts + blk - 1) // blk * blk
    pad_end = jnp.cumsum(padded)
    pad_start = pad_end - padded
    n_blocks = (n * TOP_K + e * (blk - 1) + blk - 1) // blk
    block_start = jnp.arange(n_blocks, dtype=I32) * blk
    block_expert = jnp.minimum(
        jnp.sum((pad_end[None, :] <= block_start[:, None]).astype(I32), axis=1), e - 1)
    n_active = (pad_end[-1:] // blk).astype(I32)

    dest, wk = _rank(sel, wf, pad_start.astype(F32).reshape(e, 1))
    zero_len = (padded - counts).at[e - 1].add(n_blocks * blk - pad_end[-1])
    wsgu, wsd = _cast_experts(ws_gate[:, None], ws_up[:, None], ws_down[:, None], layer)
    xs, ysh = _dispatch(h2, dest, pad_start + counts, zero_len, n_blocks * blk, wsgu, wsd)

    wgu, wd = cast if cast is not None else _cast_experts(w_gate, w_up, w_down, layer)
    ys, next_cast = _grouped_ffn(xs, block_expert, n_active, wgu, wd,
                                 (w_gate, w_up, w_down, layer + 1) if cast_next else None)

    return _combine(dest, wk.T, ys, ysh, x, gate, gain, final), next_cast


def kernel(x, c, mod_w, mod_b, lru_w_in, lru_b_in, lru_conv_w, lru_conv_b, lru_gate_a_w,
           lru_gate_a_b, lru_gate_x_w, lru_gate_x_b, lru_lambda, lru_w_out, lru_b_out, pool_w,
           pool_b, pool_scale, router_w, router_bias, expert_w_gate, expert_w_up, expert_w_down,
           shared_w_gate, shared_w_up, shared_w_down, final_gain):
    bsz, seq, d = x.shape
    depth = mod_w.shape[0]
    assert bsz == 1, "one sequence per call"
    xf = x.reshape(seq, d)
    row = lambda v: v.reshape(1, -1)

    mod = _adaln_all(c, mod_w, mod_b)

    def adaln(i, s):
        m = mod[2 * i + s]
        return m[:, :d], m[:, d:2 * d], m[:, 2 * d:]

    gain = row(final_gain)
    experts = (expert_w_gate, expert_w_up, expert_w_down)
    cast = None
    for i in range(depth):
        j = i // 2
        shift, scale, gate = adaln(i, 0)
        shift2, scale2, gate2 = adaln(i, 1)
        if i % 2 == 0:
            h = _norm_mod(xf, scale, shift)
            g, u = _inproj(h, lru_w_in[j].astype(BF16), row(lru_b_in[j]))
            z, made = _lru_scan(u, g, lru_conv_w[j], row(lru_conv_b[j]),
                                lru_gate_a_w[j].astype(BF16), row(lru_gate_a_b[j]),
                                lru_gate_x_w[j].astype(BF16), row(lru_gate_x_b[j]),
                                row(lru_lambda[j]), (*experts, i) if cast is None else None)
            cast = made if cast is None else cast
            xf, h2 = _outproj(z, lru_w_out[j].astype(BF16), row(lru_b_out[j]), xf, gate,
                              scale2, shift2)
        else:
            xf, h2 = _pool_mixer(xf, scale, shift, pool_w[j].astype(BF16), row(pool_b[j]),
                                 row(pool_scale[j]), gate, scale2, shift2)
        xf, cast = _moe(h2, xf, gate2, gain, i == depth - 1, i, cast, i + 1 < depth,
                        router_w[i], router_bias[i], *experts,
                        shared_w_gate, shared_w_up, shared_w_down)
    return xf.reshape(bsz, seq, d)
```
